```python
import math
import jax, jax.numpy as jnp
from jax import lax
import numpy as np

D_MODEL = 1024
BATCH = 8
SEQ = 2048
DEPTH = 4
DEC_BATCH = 128
DEC_SEQ = 8
PAST_LEN = 2048
PAGE_SIZE = 128

SB_HEADS = 8
SB_HEAD_DIM = 64
SB_WIDTH = SB_HEADS * SB_HEAD_DIM
DF_HEADS = 4
DF_HEAD_DIM = 64
DF_QK_DIM = 2 * DF_HEAD_DIM
DF_V_DIM = 2 * DF_HEAD_DIM
DF_WIDTH = DF_HEADS * DF_V_DIM
N_BRANCHES = 2
IN_COLS = 3 * SB_WIDTH + 2 * DF_HEADS * DF_QK_DIM + DF_WIDTH + N_BRANCHES * D_MODEL
D_FF = 128 * ((8 * D_MODEL // 3 + 127) // 128)
CONV_WIDTH = 3
Q_BLOCK = 128
RMS_EPS = 1e-6
NEG_INF = -1e30

kernel_name = 'sb_diff_gated_hybrid_step'


def rmsnorm(x, g):
    xf = x.astype(jnp.float32)
    y = xf * lax.rsqrt(jnp.mean(xf * xf, axis=-1, keepdims=True) + RMS_EPS)
    return y.astype(x.dtype) * g


def alibi_slopes(n_heads):
    return 2.0 ** (-8.0 * jnp.arange(1, n_heads + 1, dtype=jnp.float32) / n_heads)


def stick_breaking_attend(q, k, v, q_pos, k_pos):
    z = jnp.einsum('bqhd,bkhd->bhqk', q, k).astype(jnp.float32) * (SB_HEAD_DIM ** -0.5)
    mask = k_pos[None, :] < q_pos[:, None]
    log_keep = jnp.where(mask, jax.nn.log_sigmoid(-z), 0.0)
    later = lax.cumsum(log_keep, axis=3, reverse=True) - log_keep
    w = jnp.where(mask, jnp.exp(jax.nn.log_sigmoid(z) + later), 0.0)
    return jnp.einsum('bhqk,bkhd->bqhd', w.astype(v.dtype), v)


def diff_attend(q1, q2, k1, k2, v, lam, q_pos, k_pos):
    dist = (q_pos[:, None] - k_pos[None, :]).astype(jnp.float32)
    bias = -alibi_slopes(DF_HEADS)[:, None, None] * dist
    mask = k_pos[None, :] <= q_pos[:, None]
    scale = DF_HEAD_DIM ** -0.5

    def probs(q, k):
        s = jnp.einsum('bqhd,bkhd->bhqk', q, k).astype(jnp.float32) * scale + bias
        return jax.nn.softmax(jnp.where(mask, s, NEG_INF), axis=-1)

    p = probs(q1, k1) - lam.astype(jnp.float32) * probs(q2, k2)
    return jnp.einsum('bhqk,bkhd->bqhd', p.astype(v.dtype), v)


def gather_pages(cache, page_table):
    rows = cache[page_table]
    b, n, p, h, d = rows.shape
    return rows.reshape(b, n * p, h, d)


def mixer_layer(h, past, lidx, w_in, b_gate, lam_q1, lam_k1, lam_q2, lam_k2, df_norm_g,
                w_br_sb, w_br_df, w_out):
    B, T, _ = h.shape
    proj = h @ w_in
    cuts = [int(c) for c in np.cumsum([SB_WIDTH, SB_WIDTH, SB_WIDTH, DF_HEADS * DF_QK_DIM,
                                       DF_HEADS * DF_QK_DIM, DF_WIDTH])]
    sb_q, sb_k, sb_v, df_q, df_k, df_v, gates = jnp.split(proj, cuts, axis=-1)
    sb_q = sb_q.reshape(B, T, SB_HEADS, SB_HEAD_DIM)
    sb_k = sb_k.reshape(B, T, SB_HEADS, SB_HEAD_DIM)
    sb_v = sb_v.reshape(B, T, SB_HEADS, SB_HEAD_DIM)
    df_q = df_q.reshape(B, T, DF_HEADS, DF_QK_DIM)
    df_k = df_k.reshape(B, T, DF_HEADS, DF_QK_DIM)
    df_v = df_v.reshape(B, T, DF_HEADS, DF_V_DIM)

    lam_init = 0.8 - 0.6 * math.exp(-0.3 * lidx)
    lam = jnp.exp(jnp.sum(lam_q1 * lam_k1)) - jnp.exp(jnp.sum(lam_q2 * lam_k2)) + lam_init

    if past is None:
        pos = jnp.arange(T)
        sb_parts, df_parts = [], []
        for i in range(-(-T // Q_BLOCK)):
            lo, hi = i * Q_BLOCK, min((i + 1) * Q_BLOCK, T)
            sb_parts.append(stick_breaking_attend(sb_q[:, lo:hi], sb_k[:, :hi], sb_v[:, :hi],
                                                  pos[lo:hi], pos[:hi]))
            qb, kb = df_q[:, lo:hi], df_k[:, :hi]
            df_parts.append(diff_attend(qb[..., :DF_HEAD_DIM], qb[..., DF_HEAD_DIM:],
                                        kb[..., :DF_HEAD_DIM], kb[..., DF_HEAD_DIM:],
                                        df_v[:, :hi], lam, pos[lo:hi], pos[:hi]))
        y_sb = jnp.concatenate(sb_parts, axis=1)
        y_df = jnp.concatenate(df_parts, axis=1)
    else:
        past_sb_k, past_sb_v, past_df_k, past_df_v = past
        P = past_sb_k.shape[1]
        q_pos = P + jnp.arange(T)
        k_pos = jnp.arange(P + T)
        ksb = jnp.concatenate([past_sb_k, sb_k], axis=1)
        vsb = jnp.concatenate([past_sb_v, sb_v], axis=1)
        kdf = jnp.concatenate([past_df_k, df_k], axis=1)
        vdf = jnp.concatenate([past_df_v, df_v], axis=1)
        y_sb = stick_breaking_attend(sb_q, ksb, vsb, q_pos, k_pos)
        y_df = diff_attend(df_q[..., :DF_HEAD_DIM], df_q[..., DF_HEAD_DIM:],
                           kdf[..., :DF_HEAD_DIM], kdf[..., DF_HEAD_DIM:], vdf, lam, q_pos, k_pos)

    y_df = rmsnorm(y_df, df_norm_g) * (1.0 - lam_init)
    y_sb = y_sb.reshape(B, T, SB_WIDTH)
    y_df = y_df.reshape(B, T, DF_WIDTH)
    g_sb, g_df = jnp.split(jax.nn.sigmoid(gates + b_gate), N_BRANCHES, axis=-1)
    merged = g_sb * (y_sb @ w_br_sb) + g_df * (y_df @ w_br_df)
    return merged @ w_out, (sb_k, sb_v, df_k, df_v)


def conv_ffn(h, prev, w_up, conv_w, conv_b, w_down):
    T = h.shape[1]
    a, b = jnp.split(h @ w_up, 2, axis=-1)
    ap = jnp.concatenate([prev, a], axis=1)
    c = conv_b
    for j in range(CONV_WIDTH):
        c = c + conv_w[j] * ap[:, j:j + T]
    out = (jax.nn.gelu(c) * b) @ w_down
    return out, ap[:, T:]


def trunk(x, cache_sb_k, cache_sb_v, cache_df_k, cache_df_v, conv_state, page_table,
          attn_norm_g, w_in, b_gate, df_lam_q1, df_lam_k1, df_lam_q2, df_lam_k2, df_norm_g,
          w_br_sb, w_br_df, w_out, ffn_norm_g, w_up, conv_w, conv_b, w_down, final_norm_g):
    new_rows = ([], [], [], [])
    new_conv = []
    for l in range(DEPTH):
        if page_table is None:
            past = None
            prev = jnp.zeros((x.shape[0], CONV_WIDTH - 1, D_FF), x.dtype)
        else:
            past = (gather_pages(cache_sb_k[l], page_table), gather_pages(cache_sb_v[l], page_table),
                    gather_pages(cache_df_k[l], page_table), gather_pages(cache_df_v[l], page_table))
            prev = conv_state[l]
        h = rmsnorm(x, attn_norm_g[l])
        mix, rows = mixer_layer(h, past, l, w_in[l], b_gate[l], df_lam_q1[l], df_lam_k1[l],
                                df_lam_q2[l], df_lam_k2[l], df_norm_g[l], w_br_sb[l], w_br_df[l], w_out[l])
        x = x + mix
        f, c = conv_ffn(rmsnorm(x, ffn_norm_g[l]), prev, w_up[l], conv_w[l], conv_b[l], w_down[l])
        x = x + f
        for lst, r in zip(new_rows, rows):
            lst.append(r)
        new_conv.append(c)
    stacked = [jnp.stack(lst) for lst in new_rows]
    return rmsnorm(x, final_norm_g), stacked, jnp.stack(new_conv)


def setup_inputs(seed: int = 0) -> dict:
    key = jax.random.key(seed)
    ks = jax.random.split(key, 32)
    n_pages = PAST_LEN // PAGE_SIZE
    n_used = DEC_BATCH * n_pages
    n_phys = n_used + max(1, n_used // 4)
    L, D = DEPTH, D_MODEL

    def nrm(k, shape, scale):
        return jax.random.normal(k, shape, jnp.float32) * scale

    x_prompt = nrm(ks[0], (BATCH, SEQ, D), 1.0)
    x_sample = nrm(ks[1], (DEC_BATCH, DEC_SEQ, D), 1.0)
    cache_sb_k = nrm(ks[2], (L, n_phys, PAGE_SIZE, SB_HEADS, SB_HEAD_DIM), 1.0)
    cache_sb_v = nrm(ks[3], (L, n_phys, PAGE_SIZE, SB_HEADS, SB_HEAD_DIM), 1.0)
    cache_df_k = nrm(ks[4], (L, n_phys, PAGE_SIZE, DF_HEADS, DF_QK_DIM), 1.0)
    cache_df_v = nrm(ks[5], (L, n_phys, PAGE_SIZE, DF_HEADS, DF_V_DIM), 1.0)
    state_ffn_conv = nrm(ks[6], (L, DEC_BATCH, CONV_WIDTH - 1, D_FF), 1.0)
    page_table = jax.random.permutation(ks[7], n_phys)[:n_used].reshape(DEC_BATCH, n_pages).astype(jnp.int32)
    attn_norm_g = 1.0 + nrm(ks[8], (L, D), 0.02)
    w_in = nrm(ks[9], (L, D, IN_COLS), D ** -0.5)
    b_gate = nrm(ks[10], (L, N_BRANCHES * D), 0.1)
    df_lam_q1 = nrm(ks[11], (L, DF_HEAD_DIM), 0.1)
    df_lam_k1 = nrm(ks[12], (L, DF_HEAD_DIM), 0.1)
    df_lam_q2 = nrm(ks[13], (L, DF_HEAD_DIM), 0.1)
    df_lam_k2 = nrm(ks[14], (L, DF_HEAD_DIM), 0.1)
    df_norm_g = 1.0 + nrm(ks[15], (L, DF_V_DIM), 0.02)
    w_br_sb = nrm(ks[16], (L, SB_WIDTH, D), SB_WIDTH ** -0.5)
    w_br_df = nrm(ks[17], (L, DF_WIDTH, D), DF_WIDTH ** -0.5)
    w_out = nrm(ks[18], (L, D, D), D ** -0.5)
    ffn_norm_g = 1.0 + nrm(ks[19], (L, D), 0.02)
    w_up = nrm(ks[20], (L, D, 2 * D_FF), D ** -0.5)
    conv_w = nrm(ks[21], (L, CONV_WIDTH, D_FF), CONV_WIDTH ** -0.5)
    conv_b = nrm(ks[22], (L, D_FF), 0.02)
    w_down = nrm(ks[23], (L, D_FF, D), D_FF ** -0.5)
    final_norm_g = 1.0 + nrm(ks[24], (D,), 0.02)
    return {'x_prompt': x_prompt, 'x_sample': x_sample,
            'cache_sb_k': cache_sb_k, 'cache_sb_v': cache_sb_v,
            'cache_df_k': cache_df_k, 'cache_df_v': cache_df_v,
            'state_ffn_conv': state_ffn_conv, 'page_table': page_table,
            'attn_norm_g': attn_norm_g, 'w_in': w_in, 'b_gate': b_gate,
            'df_lam_q1': df_lam_q1, 'df_lam_k1': df_lam_k1, 'df_lam_q2': df_lam_q2, 'df_lam_k2': df_lam_k2,
            'df_norm_g': df_norm_g, 'w_br_sb': w_br_sb, 'w_br_df': w_br_df, 'w_out': w_out,
            'ffn_norm_g': ffn_norm_g, 'w_up': w_up, 'conv_w': conv_w, 'conv_b': conv_b,
            'w_down': w_down, 'final_norm_g': final_norm_g}


def reference(x_prompt, x_sample, cache_sb_k, cache_sb_v, cache_df_k, cache_df_v, state_ffn_conv,
              page_table, attn_norm_g, w_in, b_gate, df_lam_q1, df_lam_k1, df_lam_q2, df_lam_k2,
              df_norm_g, w_br_sb, w_br_df, w_out, ffn_norm_g, w_up, conv_w, conv_b, w_down,
              final_norm_g):
    weights = (attn_norm_g, w_in, b_gate, df_lam_q1, df_lam_k1, df_lam_q2, df_lam_k2, df_norm_g,
               w_br_sb, w_br_df, w_out, ffn_norm_g, w_up, conv_w, conv_b, w_down, final_norm_g)
    y_prompt, p_rows, p_conv = trunk(x_prompt, None, None, None, None, None, None, *weights)
    y_sample, s_rows, s_conv = trunk(x_sample, cache_sb_k, cache_sb_v, cache_df_k, cache_df_v,
                                     state_ffn_conv, page_table, *weights)
    p_sb_k, p_sb_v, p_df_k, p_df_v = p_rows
    s_sb_k, s_sb_v, s_df_k, s_df_v = s_rows
    return (y_prompt, y_sample, p_sb_k, p_sb_v, p_df_k, p_df_v, p_conv,
            s_sb_k, s_sb_v, s_df_k, s_df_v, s_conv)
```

```python
import functools
import math

import jax
import jax.numpy as jnp
from jax import lax
from jax.experimental import pallas as pl
from jax.experimental.pallas import tpu as pltpu

D_MODEL = 1024
DEPTH = 4
PAGE_SIZE = 128
SB_HEADS = 8
SB_HEAD_DIM = 64
SB_WIDTH = SB_HEADS * SB_HEAD_DIM
DF_HEADS = 4
DF_HEAD_DIM = 64
DF_QK_DIM = 2 * DF_HEAD_DIM
DF_V_DIM = 2 * DF_HEAD_DIM
DF_WIDTH = DF_HEADS * DF_V_DIM
GATE_COLS = 2 * D_MODEL
IN_COLS = 3 * SB_WIDTH + 2 * DF_HEADS * DF_QK_DIM + DF_WIDTH + GATE_COLS
D_FF = 2816
CONV_WIDTH = 3
RMS_EPS = 1e-6
NEG_INF = -1e30
SB_SCALE = SB_HEAD_DIM ** -0.5
DF_SCALE = DF_HEAD_DIM ** -0.5

LANES = 128
SUBLANES = 8
MXU_DIM = 256
VMEM_LIMIT_BYTES = 56 * 1024 * 1024

ROW_TILE = 512
STATE_ROW_TILE = 256
ATTN_BLOCK = 256
FF_CHUNK = MXU_DIM
HEAD_GROUP = LANES
N_GROUPS = SB_WIDTH // HEAD_GROUP

F32 = jnp.float32
BF16 = jnp.bfloat16

_NT = (((1,), (1,)), ((), ()))


def _params(*semantics):
    return pltpu.CompilerParams(dimension_semantics=semantics,
                                vmem_limit_bytes=VMEM_LIMIT_BYTES)


def _resident(shape, index_map):
    return pl.BlockSpec(shape, index_map, pipeline_mode=pl.Buffered(1))


def _rms(x):
    return x * lax.rsqrt(jnp.mean(x * x, axis=-1, keepdims=True) + RMS_EPS)


def _softplus(z):
    return jnp.maximum(z, 0.0) + jnp.log(1.0 + jnp.exp(-jnp.abs(z)))


def _suffix_ones(n):
    s = lax.broadcasted_iota(jnp.int32, (n, n), 0)
    j = lax.broadcasted_iota(jnp.int32, (n, n), 1)
    return jnp.where(s > j, 1.0, 0.0).astype(BF16)


def _suffix_sum(x, u):
    hi = x.astype(BF16)
    lo = (x - hi.astype(F32)).astype(BF16)
    return (jnp.dot(hi, u, preferred_element_type=F32)
            + jnp.dot(lo, u, preferred_element_type=F32))


def _lam(lq1_ref, lk1_ref, lq2_ref, lk2_ref, lam_init):
    s1 = jnp.sum(lq1_ref[...] * lk1_ref[...], axis=-1, keepdims=True)
    s2 = jnp.sum(lq2_ref[...] * lk2_ref[...], axis=-1, keepdims=True)
    return jnp.exp(s1) - jnp.exp(s2) + lam_init


def _online_softmax_step(s, m_ref, l_ref, acc_ref, idx, pv):
    m_old = m_ref[idx]
    m_new = jnp.maximum(m_old, jnp.max(s, axis=-1, keepdims=True))
    alpha = jnp.exp(m_old - m_new)
    p = jnp.exp(s - m_new)
    l_ref[idx] = alpha * l_ref[idx] + jnp.sum(p, axis=-1, keepdims=True)
    acc_ref[idx] = alpha * acc_ref[idx] + pv(p.astype(BF16))
    m_ref[idx] = m_new


def _projections(x_ref, g_ref, w_ref):
    hb = (_rms(x_ref[...]) * g_ref[...]).astype(BF16)
    bounds = [0]
    for width in (SB_WIDTH, SB_WIDTH, SB_WIDTH, DF_WIDTH, DF_WIDTH, DF_WIDTH, GATE_COLS):
        bounds.append(bounds[-1] + width)
    return [functools.partial(
        lambda lo, hi: jnp.dot(hb, w_ref[:, lo:hi], preferred_element_type=F32), lo, hi)
        for lo, hi in zip(bounds[:-1], bounds[1:])]


def _in_proj_body(x_ref, g_ref, w_ref, bg_ref,
                  qsb_ref, ksb_ref, vsb_ref, qdf_ref, kdf_ref, vdf_ref, gate_ref):
    qsb, ksb, vsb, qdf, kdf, vdf, gates = _projections(x_ref, g_ref, w_ref)
    qsb_ref[...] = (qsb() * SB_SCALE).astype(BF16)
    ksb_ref[...] = ksb()
    vsb_ref[...] = vsb()
    qdf_ref[...] = (qdf() * DF_SCALE).astype(BF16)
    kdf_ref[...] = kdf()
    vdf_ref[...] = vdf()
    gate_ref[...] = jax.nn.sigmoid(gates() + bg_ref[...])


def _in_proj(x, g, w_bf, b_gate):
    n = x.shape[0]
    tm = min(ROW_TILE, n)
    row = lambda i: (i, 0)
    fixed = lambda i: (0, 0)
    wide = lambda cols, dt: jax.ShapeDtypeStruct((n, cols), dt)
    return pl.pallas_call(
        _in_proj_body,
        grid=(n // tm,),
        in_specs=[pl.BlockSpec((tm, D_MODEL), row),
                  pl.BlockSpec((1, D_MODEL), fixed),
                  _resident((D_MODEL, IN_COLS), fixed),
                  pl.BlockSpec((1, GATE_COLS), fixed)],
        out_specs=[pl.BlockSpec((tm, SB_WIDTH), row)] * 6
                  + [pl.BlockSpec((tm, GATE_COLS), row)],
        out_shape=[wide(SB_WIDTH, BF16), wide(SB_WIDTH, F32), wide(SB_WIDTH, F32),
                   wide(DF_WIDTH, BF16), wide(DF_WIDTH, F32), wide(DF_WIDTH, F32),
                   wide(GATE_COLS, F32)],
        compiler_params=_params("arbitrary"),
        name="in_proj",
    )(x, g.reshape(1, D_MODEL), w_bf, b_gate.reshape(1, GATE_COLS))


def _in_proj_prompt_body(x_ref, g_ref, w_ref, bg_ref,
                         qsb_ref, ksb_t_ref, vsb_t_ref, ksb_tb_ref, vsb_tb_ref,
                         qdf_ref, kdf_ref, vdf_ref, kdf_b_ref, vdf_b_ref, gate_ref):
    qsb, ksb, vsb, qdf, kdf, vdf, gates = _projections(x_ref, g_ref, w_ref)
    qsb_ref[...] = (qsb() * SB_SCALE).astype(BF16)
    for proj, t_ref, tb_ref in ((ksb, ksb_t_ref, ksb_tb_ref), (vsb, vsb_t_ref, vsb_tb_ref)):
        feature_major = proj().T
        t_ref[0] = feature_major
        tb_ref[0] = feature_major.astype(BF16)
    qdf_ref[...] = (qdf() * DF_SCALE).astype(BF16)
    for proj, h_ref, b_ref in ((kdf, kdf_ref, kdf_b_ref), (vdf, vdf_ref, vdf_b_ref)):
        rows = proj()
        for h in range(DF_HEADS):
            h_ref[0, :, h, :] = rows[:, h * DF_V_DIM:(h + 1) * DF_V_DIM]
        b_ref[...] = rows.astype(BF16)
    gate_ref[...] = jax.nn.sigmoid(gates() + bg_ref[...])


def _in_proj_prompt(x, g, w_bf, b_gate, batch, seq_len):
    n = batch * seq_len
    tm = ROW_TILE
    tiles = seq_len // tm
    row = lambda b, t: (b * tiles + t, 0)
    fixed = lambda b, t: (0, 0)
    feat = pl.BlockSpec((1, SB_WIDTH, tm), lambda b, t: (b, 0, t))
    heads = pl.BlockSpec((1, tm, DF_HEADS, DF_V_DIM), lambda b, t: (b, t, 0, 0))
    rows = pl.BlockSpec((tm, SB_WIDTH), row)
    feat_shape = lambda dt: jax.ShapeDtypeStruct((batch, SB_WIDTH, seq_len), dt)
    head_shape = jax.ShapeDtypeStruct((batch, seq_len, DF_HEADS, DF_V_DIM), F32)
    wide = lambda cols, dt: jax.ShapeDtypeStruct((n, cols), dt)
    return pl.pallas_call(
        _in_proj_prompt_body,
        grid=(batch, tiles),
        in_specs=[pl.BlockSpec((tm, D_MODEL), row),
                  pl.BlockSpec((1, D_MODEL), fixed),
                  _resident((D_MODEL, IN_COLS), fixed),
                  pl.BlockSpec((1, GATE_COLS), fixed)],
        out_specs=[rows, feat, feat, feat, feat, rows, heads, heads, rows, rows,
                   pl.BlockSpec((tm, GATE_COLS), row)],
        out_shape=[wide(SB_WIDTH, BF16), feat_shape(F32), feat_shape(F32),
                   feat_shape(BF16), feat_shape(BF16),
                   wide(DF_WIDTH, BF16), head_shape, head_shape,
                   wide(DF_WIDTH, BF16), wide(DF_WIDTH, BF16),
                   wide(GATE_COLS, F32)],
        compiler_params=_params("arbitrary", "arbitrary"),
        name="in_proj_prompt",
    )(x, g.reshape(1, D_MODEL), w_bf, b_gate.reshape(1, GATE_COLS))


def _prompt_attn_body(slopes_ref, lq1_ref, lk1_ref, lq2_ref, lk2_ref, gn_ref,
                      qsb_ref, ksb_t_ref, vsb_t_ref, qdf_ref, kdf_ref, vdf_ref,
                      ysb_ref, ydf_ref,
                      acc_sb, run_sb, acc_df, m_df, l_df, *, lam_init):
    blk = ATTN_BLOCK
    hd = SB_HEAD_DIM
    grp = pl.program_id(1)
    qi = pl.program_id(2)
    slope = slopes_ref[grp]

    r = lax.broadcasted_iota(jnp.int32, (blk, blk), 0)
    c = lax.broadcasted_iota(jnp.int32, (blk, blk), 1)
    strict = c < r
    causal = c <= r
    alibi_local = (-slope) * (r - c).astype(F32)
    u = _suffix_ones(blk)

    acc_sb[...] = jnp.zeros_like(acc_sb)
    run_sb[...] = jnp.zeros_like(run_sb)
    acc_df[...] = jnp.zeros_like(acc_df)
    m_df[...] = jnp.full_like(m_df, NEG_INF)
    l_df[...] = jnp.zeros_like(l_df)

    def block(kj, diagonal):
        ks = pl.multiple_of(kj * blk, blk)
        for h in range(HEAD_GROUP // hd):
            feat = slice(h * hd, (h + 1) * hd)
            z = jnp.dot(qsb_ref[0, :, feat], ksb_t_ref[0, feat, pl.ds(ks, blk)],
                        preferred_element_type=F32)
            sp = _softplus(z)
            log_keep = jnp.where(strict, -sp, 0.0) if diagonal else -sp
            later = _suffix_sum(log_keep, u) + run_sb[h]
            w = jnp.exp((z - sp) + later)
            if diagonal:
                w = jnp.where(strict, w, 0.0)
            acc_sb[h] += lax.dot_general(w.astype(BF16), vsb_t_ref[0, feat, pl.ds(ks, blk)],
                                         _NT, preferred_element_type=F32)
            run_sb[h] += jnp.sum(log_keep, axis=-1, keepdims=True)
        k2 = kdf_ref[0, pl.ds(ks, blk), :]
        v2 = vdf_ref[0, pl.ds(ks, blk), :]
        bias = alibi_local - slope * ((qi - kj) * blk).astype(F32)
        for m in range(2):
            cols = slice(m * DF_HEAD_DIM, (m + 1) * DF_HEAD_DIM)
            s = lax.dot_general(qdf_ref[0, :, cols], k2[:, cols], _NT,
                                preferred_element_type=F32) + bias
            if diagonal:
                s = jnp.where(causal, s, NEG_INF)
            _online_softmax_step(
                s, m_df, l_df, acc_df, m,
                lambda p: jnp.dot(p, v2, preferred_element_type=F32))

    block(qi, True)

    def older(i, carry):
        block(qi - 1 - i, False)
        return carry

    lax.fori_loop(0, qi, older, 0)

    for h in range(HEAD_GROUP // hd):
        ysb_ref[0, :, h * hd:(h + 1) * hd] = acc_sb[h].astype(ysb_ref.dtype)
    lam = _lam(lq1_ref, lk1_ref, lq2_ref, lk2_ref, lam_init)
    o = acc_df[0] / l_df[0] - lam * (acc_df[1] / l_df[1])
    ydf_ref[0] = ((_rms(o) * gn_ref[...]) * (1.0 - lam_init)).astype(ydf_ref.dtype)


def _prompt_attention(slopes, lam_vecs, gn, qsb, ksb_t, vsb_t, qdf, kdf, vdf, lam_init):
    b, t, _ = qsb.shape
    blk = ATTN_BLOCK
    qmap = lambda bi, g, qi: (bi, qi, g)
    fixed = lambda bi, g, qi: (0, 0)
    vec = pl.BlockSpec((1, DF_HEAD_DIM), fixed)
    qspec = pl.BlockSpec((1, blk, HEAD_GROUP), qmap)
    rows = pl.BlockSpec((1, t, HEAD_GROUP), lambda bi, g, qi: (bi, 0, g))
    feat = pl.BlockSpec((1, HEAD_GROUP, t), lambda bi, g, qi: (bi, g, 0))
    return pl.pallas_call(
        functools.partial(_prompt_attn_body, lam_init=lam_init),
        grid=(b, N_GROUPS, t // blk),
        in_specs=[pl.BlockSpec(memory_space=pltpu.SMEM), vec, vec, vec, vec,
                  pl.BlockSpec((1, DF_V_DIM), fixed),
                  qspec, feat, feat, qspec, rows, rows],
        out_specs=[qspec, qspec],
        out_shape=[jax.ShapeDtypeStruct((b, t, SB_WIDTH), BF16),
                   jax.ShapeDtypeStruct((b, t, DF_WIDTH), BF16)],
        scratch_shapes=[pltpu.VMEM((2, blk, SB_HEAD_DIM), F32),
                        pltpu.VMEM((2, blk, 1), F32),
                        pltpu.VMEM((2, blk, DF_V_DIM), F32),
                        pltpu.VMEM((2, blk, 1), F32),
                        pltpu.VMEM((2, blk, 1), F32)],
        compiler_params=_params("arbitrary", "arbitrary", "arbitrary"),
        name="prompt_attention",
    )(slopes, *lam_vecs, gn, qsb, ksb_t, vsb_t, qdf, kdf, vdf)


def _decode_attn_body(pt_ref, slopes_ref, lq1_ref, lk1_ref, lq2_ref, lk2_ref, gn_ref,
                      qsb_ref, ksb_new_ref, vsb_new_ref, qdf_ref, kdf_new_ref, vdf_new_ref,
                      ksb_pg_ref, vsb_pg_ref, kdf_pg_ref, vdf_pg_ref,
                      ysb_ref, ydf_ref,
                      qbd_sb, q_df, acc_sb, run_sb, acc_df, m_df, l_df,
                      *, lam_init, past_len, n_pages, t_new):
    del pt_ref
    step = pl.program_id(1)
    rows = SB_HEADS * t_new
    df_rows = 2 * t_new
    pg = PAGE_SIZE

    row_i = lax.broadcasted_iota(jnp.int32, (rows, pg), 0)
    col_i = lax.broadcasted_iota(jnp.int32, (rows, pg), 1)
    tok = row_i % t_new
    u = _suffix_ones(pg)
    df_head = lax.broadcasted_iota(jnp.int32, (rows, 1), 0) // df_rows
    slope = jnp.zeros((rows, 1), F32)
    for h in range(DF_HEADS):
        slope = jnp.where(df_head == h, slopes_ref[h], slope)

    def attend(z, sb_pv, df_k, df_v, first_pos, fresh):
        sp = _softplus(z)
        log_keep = jnp.where(col_i < tok, -sp, 0.0) if fresh else -sp
        later = _suffix_sum(log_keep, u) + run_sb[...]
        w = jnp.exp((z - sp) + later)
        if fresh:
            w = jnp.where(col_i < tok, w, 0.0)
        acc_sb[...] += sb_pv(w.astype(BF16))
        run_sb[...] += jnp.sum(log_keep, axis=-1, keepdims=True)
        s = jnp.concatenate(
            [lax.dot_general(q_df[h * df_rows:(h + 1) * df_rows, :], df_k[h], _NT,
                             preferred_element_type=F32) for h in range(DF_HEADS)], axis=0)
        s = s - slope * (past_len - first_pos + tok - col_i).astype(F32)
        if fresh:
            s = jnp.where(col_i <= tok, s, NEG_INF)

        def df_pv(p):
            return jnp.concatenate(
                [jnp.dot(p[h * df_rows:(h + 1) * df_rows, :], df_v[h],
                         preferred_element_type=F32) for h in range(DF_HEADS)], axis=0)

        _online_softmax_step(s, m_df, l_df, acc_df, slice(None), df_pv)

    @pl.when(step == 0)
    def _():
        r = lax.broadcasted_iota(jnp.int32, (rows, SB_WIDTH), 0) // t_new
        cb = lax.broadcasted_iota(jnp.int32, (rows, SB_WIDTH), 1) // SB_HEAD_DIM
        q = qsb_ref[0].astype(F32)
        qbd_sb[...] = jnp.where(r == cb, jnp.concatenate([q] * SB_HEADS, axis=0),
                                0.0).astype(BF16)
        q = qdf_ref[0].astype(F32)
        first_half = lax.broadcasted_iota(jnp.int32, (t_new, DF_QK_DIM), 1) < DF_HEAD_DIM
        parts = []
        for h in range(DF_HEADS):
            qh = q[:, h * DF_QK_DIM:(h + 1) * DF_QK_DIM]
            parts += [jnp.where(first_half, qh, 0.0), jnp.where(first_half, 0.0, qh)]
        q_df[...] = jnp.concatenate(parts, axis=0).astype(BF16)
        acc_sb[...] = jnp.zeros_like(acc_sb)
        run_sb[...] = jnp.zeros_like(run_sb)
        acc_df[...] = jnp.zeros_like(acc_df)
        m_df[...] = jnp.full_like(m_df, NEG_INF)
        l_df[...] = jnp.zeros_like(l_df)

        def padded(ref):
            pad = jnp.zeros((pg - t_new, ref.shape[-1]), F32)
            return jnp.concatenate([ref[0], pad], axis=0).astype(BF16)

        ksb, vsb, kdf, vdf = (padded(ksb_new_ref), padded(vsb_new_ref),
                              padded(kdf_new_ref), padded(vdf_new_ref))
        per_head = lambda a: [a[:, h * DF_V_DIM:(h + 1) * DF_V_DIM] for h in range(DF_HEADS)]
        attend(lax.dot_general(qbd_sb[...], ksb, _NT, preferred_element_type=F32),
               lambda w: jnp.dot(w, vsb, preferred_element_type=F32),
               per_head(kdf), per_head(vdf), past_len, True)

    @pl.when(step > 0)
    def _():
        page = n_pages - step
        vsb_t = vsb_pg_ref[0, 0].astype(BF16)
        attend(jnp.dot(qbd_sb[...], ksb_pg_ref[0, 0].astype(BF16), preferred_element_type=F32),
               lambda w: lax.dot_general(w, vsb_t, _NT, preferred_element_type=F32),
               [kdf_pg_ref[0, 0, :, h, :].astype(BF16) for h in range(DF_HEADS)],
               [vdf_pg_ref[0, 0, :, h, :].astype(BF16) for h in range(DF_HEADS)],
               page * pg, False)

    @pl.when(step == n_pages)
    def _():
        r = lax.broadcasted_iota(jnp.int32, (rows, SB_WIDTH), 0) // t_new
        cb = lax.broadcasted_iota(jnp.int32, (rows, SB_WIDTH), 1) // SB_HEAD_DIM
        ysb = jnp.where(r == cb, acc_sb[...], 0.0).reshape(SB_HEADS, t_new, SB_WIDTH).sum(axis=0)
        ysb_ref[0] = ysb.astype(ysb_ref.dtype)

        lam = _lam(lq1_ref, lk1_ref, lq2_ref, lk2_ref, lam_init)
        normed = acc_df[...] / l_df[...]
        outs = []
        for h in range(DF_HEADS):
            first = normed[h * df_rows:h * df_rows + t_new, :]
            second = normed[h * df_rows + t_new:(h + 1) * df_rows, :]
            o = first - lam * second
            outs.append((_rms(o) * gn_ref[...]) * (1.0 - lam_init))
        ydf_ref[0] = jnp.concatenate(outs, axis=-1).astype(ydf_ref.dtype)


def _decode_attention(page_table, slopes, lam_vecs, gn, qsb, ksb, vsb, qdf, kdf, vdf,
                      cache_sb_k_t, cache_sb_v_t, cache_df_k, cache_df_v, layer, lam_init):
    nb, t_new, _ = qsb.shape
    n_pages = page_table.shape[1]
    past_len = n_pages * PAGE_SIZE
    rows = SB_HEADS * t_new
    new = pl.BlockSpec((1, t_new, SB_WIDTH), lambda bi, s, pt: (bi, 0, 0))
    fixed = lambda bi, s, pt: (0, 0)
    vec = pl.BlockSpec((1, DF_HEAD_DIM), fixed)

    def page_of(bi, s, pt):
        return pt[bi, n_pages - jnp.maximum(s, 1)]

    sb_page = pl.BlockSpec((1, 1, SB_WIDTH, PAGE_SIZE),
                           lambda bi, s, pt: (layer, page_of(bi, s, pt), 0, 0))
    df_page = pl.BlockSpec((1, 1, PAGE_SIZE, DF_HEADS, DF_V_DIM),
                           lambda bi, s, pt: (layer, page_of(bi, s, pt), 0, 0, 0))
    grid_spec = pltpu.PrefetchScalarGridSpec(
        num_scalar_prefetch=1,
        grid=(nb, n_pages + 1),
        in_specs=[pl.BlockSpec(memory_space=pltpu.SMEM), vec, vec, vec, vec,
                  pl.BlockSpec((1, DF_V_DIM), fixed),
                  new, new, new, new, new, new, sb_page, sb_page, df_page, df_page],
        out_specs=[new, new],
        scratch_shapes=[pltpu.VMEM((rows, SB_WIDTH), BF16),
                        pltpu.VMEM((rows, DF_QK_DIM), BF16),
                        pltpu.VMEM((rows, SB_WIDTH), F32),
                        pltpu.VMEM((rows, 1), F32),
                        pltpu.VMEM((rows, DF_V_DIM), F32),
                        pltpu.VMEM((rows, 1), F32),
                        pltpu.VMEM((rows, 1), F32)],
    )
    return pl.pallas_call(
        functools.partial(_decode_attn_body, lam_init=lam_init, past_len=past_len,
                          n_pages=n_pages, t_new=t_new),
        grid_spec=grid_spec,
        out_shape=[jax.ShapeDtypeStruct((nb, t_new, SB_WIDTH), BF16),
                   jax.ShapeDtypeStruct((nb, t_new, DF_WIDTH), BF16)],
        compiler_params=_params("arbitrary", "arbitrary"),
        name="decode_attention",
    )(page_table, slopes, *lam_vecs, gn, qsb, ksb, vsb, qdf, kdf, vdf,
      cache_sb_k_t, cache_sb_v_t, cache_df_k, cache_df_v)


def _merge_out_body(x_ref, ysb_ref, ydf_ref, gate_ref, wsb_ref, wdf_ref, wout_ref, o_ref):
    br_sb = jnp.dot(ysb_ref[...], wsb_ref[...], preferred_element_type=F32)
    br_df = jnp.dot(ydf_ref[...], wdf_ref[...], preferred_element_type=F32)
    merged = gate_ref[:, :D_MODEL] * br_sb + gate_ref[:, D_MODEL:] * br_df
    o_ref[...] = x_ref[...] + jnp.dot(merged.astype(BF16), wout_ref[...],
                                      preferred_element_type=F32)


def _merge_out(x, ysb, ydf, gates, wsb_bf, wdf_bf, wout_bf):
    n = x.shape[0]
    tm = min(ROW_TILE, n)
    row = lambda i: (i, 0)
    fixed = lambda i: (0, 0)
    return pl.pallas_call(
        _merge_out_body,
        grid=(n // tm,),
        in_specs=[pl.BlockSpec((tm, D_MODEL), row),
                  pl.BlockSpec((tm, SB_WIDTH), row),
                  pl.BlockSpec((tm, DF_WIDTH), row),
                  pl.BlockSpec((tm, GATE_COLS), row),
                  _resident((SB_WIDTH, D_MODEL), fixed),
                  _resident((DF_WIDTH, D_MODEL), fixed),
                  _resident((D_MODEL, D_MODEL), fixed)],
        out_specs=pl.BlockSpec((tm, D_MODEL), row),
        out_shape=jax.ShapeDtypeStruct((n, D_MODEL), F32),
        compiler_params=_params("arbitrary"),
        name="merge_out",
    )(x, ysb, ydf, gates, wsb_bf, wdf_bf, wout_bf)


def _conv_ffn_body(*refs, seq_len, tile_rows, has_state):
    if has_state:
        (x_ref, g_ref, wup_ref, cw_ref, cb_ref, wdn_ref, e1_ref, e2_ref,
         o_ref, tail_ref) = refs
    else:
        (x_ref, g_ref, wup_ref, cw_ref, cb_ref, wdn_ref,
         o_ref, tail_ref, carry) = refs

        @pl.when(pl.program_id(1) == 0)
        def _():
            carry[...] = jnp.zeros_like(carry)

    x = x_ref[...]
    hb = (_rms(x) * g_ref[...]).astype(BF16)
    row = lax.broadcasted_iota(jnp.int32, (tile_rows, FF_CHUNK), 0)
    t_in_seq = row % seq_len if has_state else row
    acc = x
    for ci in range(D_FF // FF_CHUNK):
        cols = slice(ci * FF_CHUNK, (ci + 1) * FF_CHUNK)
        gcols = slice(D_FF + ci * FF_CHUNK, D_FF + (ci + 1) * FF_CHUNK)
        a = jnp.dot(hb, wup_ref[:, cols], preferred_element_type=F32)
        b = jnp.dot(hb, wup_ref[:, gcols], preferred_element_type=F32)
        a1 = pltpu.roll(a, 1, 0)
        a2 = pltpu.roll(a, 2, 0)
        if has_state:
            a1 = jnp.where(t_in_seq == 0, e1_ref[:, cols], a1)
            a2 = jnp.where(t_in_seq < 2, e2_ref[:, cols], a2)
            tail_ref[:, cols] = a
        else:
            last = carry[SUBLANES - 1:SUBLANES, cols]
            prev = carry[SUBLANES - 2:SUBLANES - 1, cols]
            a1 = jnp.where(t_in_seq == 0, last, a1)
            a2 = jnp.where(t_in_seq == 0, prev, jnp.where(t_in_seq == 1, last, a2))
            tail = a[tile_rows - SUBLANES:, :]
            carry[:, cols] = tail
            tail_ref[0, :, cols] = tail
        conv = cb_ref[:, cols] + cw_ref[0:1, cols] * a2
        conv = conv + cw_ref[1:2, cols] * a1
        conv = conv + cw_ref[2:3, cols] * a
        gated = (jax.nn.gelu(conv) * b).astype(BF16)
        acc = acc + jnp.dot(gated, wdn_ref[cols, :], preferred_element_type=F32)
    o_ref[...] = acc


def _conv_ffn(x, g, wup_bf, conv_w, conv_b, wdn_bf, batch, seq_len, state):
    n = x.shape[0]
    has_state = state is not None
    tm = STATE_ROW_TILE if has_state else ROW_TILE
    body = functools.partial(_conv_ffn_body, seq_len=seq_len, has_state=has_state,
                             tile_rows=tm)
    weights = [g.reshape(1, D_MODEL), wup_bf, conv_w, conv_b.reshape(1, D_FF), wdn_bf]
    if has_state:
        pad = lambda s: jnp.pad(s, ((0, 0), (0, seq_len - s.shape[1]), (0, 0))).reshape(n, D_FF)
        e1 = pad(state[:, 1:2])
        e2 = pad(state)
        row = lambda i: (i, 0)
        fixed = lambda i: (0, 0)
        out, gate_rows = pl.pallas_call(
            body,
            grid=(n // tm,),
            in_specs=[pl.BlockSpec((tm, D_MODEL), row),
                      pl.BlockSpec((1, D_MODEL), fixed),
                      _resident((D_MODEL, 2 * D_FF), fixed),
                      pl.BlockSpec((CONV_WIDTH, D_FF), fixed),
                      pl.BlockSpec((1, D_FF), fixed),
                      _resident((D_FF, D_MODEL), fixed),
                      pl.BlockSpec((tm, D_FF), row),
                      pl.BlockSpec((tm, D_FF), row)],
            out_specs=[pl.BlockSpec((tm, D_MODEL), row), pl.BlockSpec((tm, D_FF), row)],
            out_shape=[jax.ShapeDtypeStruct((n, D_MODEL), F32),
                       jax.ShapeDtypeStruct((n, D_FF), F32)],
            compiler_params=_params("arbitrary"),
            name="conv_ffn_decode",
        )(x, *weights, e1, e2)
        tail = gate_rows.reshape(batch, seq_len, D_FF)[:, seq_len - (CONV_WIDTH - 1):]
        return out, tail
    tiles = seq_len // tm
    row = lambda bi, ti: (bi * tiles + ti, 0)
    fixed = lambda bi, ti: (0, 0)
    out, tail8 = pl.pallas_call(
        body,
        grid=(batch, tiles),
        in_specs=[pl.BlockSpec((tm, D_MODEL), row),
                  pl.BlockSpec((1, D_MODEL), fixed),
                  _resident((D_MODEL, 2 * D_FF), fixed),
                  pl.BlockSpec((CONV_WIDTH, D_FF), fixed),
                  pl.BlockSpec((1, D_FF), fixed),
                  _resident((D_FF, D_MODEL), fixed)],
        out_specs=[pl.BlockSpec((tm, D_MODEL), row),
                   pl.BlockSpec((1, SUBLANES, D_FF), lambda bi, ti: (bi, 0, 0))],
        out_shape=[jax.ShapeDtypeStruct((n, D_MODEL), F32),
                   jax.ShapeDtypeStruct((batch, SUBLANES, D_FF), F32)],
        scratch_shapes=[pltpu.VMEM((SUBLANES, D_FF), F32)],
        compiler_params=_params("arbitrary", "arbitrary"),
        name="conv_ffn_prompt",
    )(x, *weights)
    return out, tail8[:, SUBLANES - (CONV_WIDTH - 1):]


def _final_norm_body(x_ref, g_ref, o_ref):
    o_ref[...] = _rms(x_ref[...]) * g_ref[...]


def _final_norm(x, g):
    n = x.shape[0]
    tm = min(ROW_TILE, n)
    return pl.pallas_call(
        _final_norm_body,
        grid=(n // tm,),
        in_specs=[pl.BlockSpec((tm, D_MODEL), lambda i: (i, 0)),
                  pl.BlockSpec((1, D_MODEL), lambda i: (0, 0))],
        out_specs=pl.BlockSpec((tm, D_MODEL), lambda i: (i, 0)),
        out_shape=jax.ShapeDtypeStruct((n, D_MODEL), F32),
        compiler_params=_params("arbitrary"),
        name="final_norm",
    )(x, g.reshape(1, D_MODEL))


def _layer_consts(wts, l):
    lam_init = 0.8 - 0.6 * math.exp(-0.3 * l)
    lam_vecs = [wts[k][l].reshape(1, DF_HEAD_DIM)
                for k in ("df_lam_q1", "df_lam_k1", "df_lam_q2", "df_lam_k2")]
    return lam_init, lam_vecs, wts["df_norm_g"][l].reshape(1, DF_V_DIM)


def _mix_and_ffn(xf, ysb, ydf, gates, wts, l, batch, seq_len, state):
    n = xf.shape[0]
    xf = _merge_out(xf, ysb.reshape(n, SB_WIDTH), ydf.reshape(n, DF_WIDTH), gates,
                    wts["w_br_sb"][l], wts["w_br_df"][l], wts["w_out"][l])
    return _conv_ffn(xf, wts["ffn_norm_g"][l], wts["w_up"][l], wts["conv_w"][l],
                     wts["conv_b"][l], wts["w_down"][l], batch, seq_len, state)


def _prompt_trunk(x, slopes, wts):
    batch, seq_len, _ = x.shape
    xf = x.reshape(batch * seq_len, D_MODEL)
    rows = ([], [], [], [])
    convs = []
    for l in range(DEPTH):
        lam_init, lam_vecs, gn = _layer_consts(wts, l)
        (qsb, ksb_t, vsb_t, ksb_tb, vsb_tb, qdf, kdf, vdf, kdf_b, vdf_b, gates) = _in_proj_prompt(
            xf, wts["attn_norm_g"][l], wts["w_in"][l], wts["b_gate"][l], batch, seq_len)
        tok = lambda a: a.reshape(batch, seq_len, SB_WIDTH)
        ysb, ydf = _prompt_attention(slopes, lam_vecs, gn, tok(qsb), ksb_tb, vsb_tb,
                                     tok(qdf), tok(kdf_b), tok(vdf_b), lam_init)
        xf, tail = _mix_and_ffn(xf, ysb, ydf, gates, wts, l, batch, seq_len, None)
        for lst, a in zip(rows, (ksb_t, vsb_t, kdf, vdf)):
            lst.append(a)
        convs.append(tail)
    y = _final_norm(xf, wts["final_norm_g"]).reshape(batch, seq_len, D_MODEL)
    to_tokens = lambda a: a.reshape(DEPTH, batch, SB_HEADS, SB_HEAD_DIM, seq_len).transpose(0, 1, 4, 2, 3)
    return (y, to_tokens(jnp.stack(rows[0])), to_tokens(jnp.stack(rows[1])),
            jnp.stack(rows[2]), jnp.stack(rows[3]), jnp.stack(convs))


def _decode_trunk(x, caches, conv_state, page_table, slopes, wts):
    batch, seq_len, _ = x.shape
    n = batch * seq_len
    xf = x.reshape(n, D_MODEL)
    depth, n_phys = caches[0].shape[:2]
    sb_t = lambda c: c.transpose(0, 1, 3, 4, 2).reshape(depth, n_phys, SB_WIDTH, PAGE_SIZE)
    caches = (sb_t(caches[0]), sb_t(caches[1]), caches[2], caches[3])
    rows = ([], [], [], [])
    convs = []
    for l in range(DEPTH):
        lam_init, lam_vecs, gn = _layer_consts(wts, l)
        qsb, ksb, vsb, qdf, kdf, vdf, gates = _in_proj(
            xf, wts["attn_norm_g"][l], wts["w_in"][l], wts["b_gate"][l])
        tok = lambda a: a.reshape(batch, seq_len, SB_WIDTH)
        ysb, ydf = _decode_attention(page_table, slopes, lam_vecs, gn, tok(qsb), tok(ksb),
                                     tok(vsb), tok(qdf), tok(kdf), tok(vdf), *caches, l, lam_init)
        xf, tail = _mix_and_ffn(xf, ysb, ydf, gates, wts, l, batch, seq_len, conv_state[l])
        rows[0].append(ksb.reshape(batch, seq_len, SB_HEADS, SB_HEAD_DIM))
        rows[1].append(vsb.reshape(batch, seq_len, SB_HEADS, SB_HEAD_DIM))
        rows[2].append(kdf.reshape(batch, seq_len, DF_HEADS, DF_QK_DIM))
        rows[3].append(vdf.reshape(batch, seq_len, DF_HEADS, DF_V_DIM))
        convs.append(tail)
    y = _final_norm(xf, wts["final_norm_g"]).reshape(batch, seq_len, D_MODEL)
    return (y, *[jnp.stack(r) for r in rows], jnp.stack(convs))


def kernel(x_prompt, x_sample, cache_sb_k, cache_sb_v, cache_df_k, cache_df_v, state_ffn_conv,
           page_table, attn_norm_g, w_in, b_gate, df_lam_q1, df_lam_k1, df_lam_q2, df_lam_k2,
           df_norm_g, w_br_sb, w_br_df, w_out, ffn_norm_g, w_up, conv_w, conv_b, w_down,
           final_norm_g):
    wts = dict(attn_norm_g=attn_norm_g, w_in=w_in.astype(BF16), b_gate=b_gate,
               df_lam_q1=df_lam_q1, df_lam_k1=df_lam_k1, df_lam_q2=df_lam_q2,
               df_lam_k2=df_lam_k2, df_norm_g=df_norm_g,
               w_br_sb=w_br_sb.astype(BF16), w_br_df=w_br_df.astype(BF16),
               w_out=w_out.astype(BF16), ffn_norm_g=ffn_norm_g, w_up=w_up.astype(BF16),
               conv_w=conv_w, conv_b=conv_b, w_down=w_down.astype(BF16),
               final_norm_g=final_norm_g)
    slopes = 2.0 ** (-8.0 * jnp.arange(1, DF_HEADS + 1, dtype=F32) / DF_HEADS)
    caches = (cache_sb_k, cache_sb_v, cache_df_k, cache_df_v)
    y_p, *rest_p = _prompt_trunk(x_prompt, slopes, wts)
    y_s, *rest_s = _decode_trunk(x_sample, caches, state_ffn_conv, page_table, slopes, wts)
    return (y_p, y_s, *rest_p, *rest_s)
```

```python
import functools
import math

import jax
import jax.numpy as jnp
from jax import lax
from jax.experimental import pallas as pl
from jax.experimental.pallas import tpu as pltpu

D_MODEL = 1024
DEPTH = 4
PAGE_SIZE = 128
SB_HEADS = 8
SB_HEAD_DIM = 64
SB_WIDTH = SB_HEADS * SB_HEAD_DIM
DF_HEADS = 4
DF_HEAD_DIM = 64
DF_QK_DIM = 2 * DF_HEAD_DIM
DF_V_DIM = 2 * DF_HEAD_DIM
DF_WIDTH = DF_HEADS * DF_V_DIM
GATE_COLS = 2 * D_MODEL
IN_COLS = 3 * SB_WIDTH + 2 * DF_HEADS * DF_QK_DIM + DF_WIDTH + GATE_COLS
D_FF = 2816
CONV_WIDTH = 3
RMS_EPS = 1e-6
NEG_INF = -1e30
SB_SCALE = SB_HEAD_DIM ** -0.5
DF_SCALE = DF_HEAD_DIM ** -0.5

LANES = 128
SUBLANES = 8
MXU_DIM = 256
VMEM_LIMIT_BYTES = 56 * 1024 * 1024

ROW_TILE = 512
STATE_ROW_TILE = 256
ATTN_BLOCK = 256
ATTN_SPAN = 4
FF_CHUNK = MXU_DIM
PAGES_PER_STEP = 8
HEAD_GROUP = LANES
N_GROUPS = SB_WIDTH // HEAD_GROUP

F32 = jnp.float32
BF16 = jnp.bfloat16

_NT = (((1,), (1,)), ((), ()))


def _params(*semantics):
    return pltpu.CompilerParams(dimension_semantics=semantics,
                                vmem_limit_bytes=VMEM_LIMIT_BYTES)


def _resident(shape, index_map):
    return pl.BlockSpec(shape, index_map, pipeline_mode=pl.Buffered(1))


def _rms(x):
    return x * lax.rsqrt(jnp.mean(x * x, axis=-1, keepdims=True) + RMS_EPS)


def _softplus(z):
    return jnp.maximum(z, 0.0) + jnp.log(1.0 + jnp.exp(-jnp.abs(z)))


def _suffix_ones(n):
    s = lax.broadcasted_iota(jnp.int32, (n, n), 0)
    j = lax.broadcasted_iota(jnp.int32, (n, n), 1)
    return jnp.where(s > j, 1.0, 0.0).astype(BF16)


def _suffix_sum(x, u):
    hi = x.astype(BF16)
    lo = (x - hi.astype(F32)).astype(BF16)
    return (jnp.dot(hi, u, preferred_element_type=F32)
            + jnp.dot(lo, u, preferred_element_type=F32))


def _lam(lq1_ref, lk1_ref, lq2_ref, lk2_ref, lam_init):
    s1 = jnp.sum(lq1_ref[...] * lk1_ref[...], axis=-1, keepdims=True)
    s2 = jnp.sum(lq2_ref[...] * lk2_ref[...], axis=-1, keepdims=True)
    return jnp.exp(s1) - jnp.exp(s2) + lam_init


def _sb_log_terms(z, u, valid):
    sp = _softplus(z)
    log_keep = -sp if valid is None else jnp.where(valid, -sp, 0.0)
    return z - sp, _suffix_sum(log_keep, u), jnp.sum(log_keep, axis=-1, keepdims=True)


def _sb_weights(terms, run, valid):
    log_hit, later, _ = terms
    w = jnp.exp(log_hit + (later + run))
    if valid is not None:
        w = jnp.where(valid, w, 0.0)
    return w.astype(BF16)


def _softmax_probs(s, m_ref, l_ref, idx):
    m_old = m_ref[idx]
    m_new = jnp.maximum(m_old, jnp.max(s, axis=-1, keepdims=True))
    alpha = jnp.exp(m_old - m_new)
    p = jnp.exp(s - m_new)
    l_ref[idx] = alpha * l_ref[idx] + jnp.sum(p, axis=-1, keepdims=True)
    m_ref[idx] = m_new
    return alpha, p.astype(BF16)


def _projections(x_ref, g_ref, w_ref):
    hb = (_rms(x_ref[...]) * g_ref[...]).astype(BF16)
    bounds = [0]
    for width in (SB_WIDTH, SB_WIDTH, SB_WIDTH, DF_WIDTH, DF_WIDTH, DF_WIDTH, GATE_COLS):
        bounds.append(bounds[-1] + width)
    return [functools.partial(
        lambda lo, hi: jnp.dot(hb, w_ref[:, lo:hi], preferred_element_type=F32), lo, hi)
        for lo, hi in zip(bounds[:-1], bounds[1:])]


def _in_proj_body(x_ref, g_ref, w_ref, bg_ref,
                  qsb_ref, ksb_ref, vsb_ref, qdf_ref, kdf_ref, vdf_ref, gate_ref):
    qsb, ksb, vsb, qdf, kdf, vdf, gates = _projections(x_ref, g_ref, w_ref)
    qsb_ref[...] = (qsb() * SB_SCALE).astype(BF16)
    ksb_ref[...] = ksb()
    vsb_ref[...] = vsb()
    qdf_ref[...] = (qdf() * DF_SCALE).astype(BF16)
    kdf_ref[...] = kdf()
    vdf_ref[...] = vdf()
    gate_ref[...] = jax.nn.sigmoid(gates() + bg_ref[...])


def _in_proj(x, g, w_bf, b_gate):
    n = x.shape[0]
    tm = min(ROW_TILE, n)
    row = lambda i: (i, 0)
    fixed = lambda i: (0, 0)
    wide = lambda cols, dt: jax.ShapeDtypeStruct((n, cols), dt)
    return pl.pallas_call(
        _in_proj_body,
        grid=(n // tm,),
        in_specs=[pl.BlockSpec((tm, D_MODEL), row),
                  pl.BlockSpec((1, D_MODEL), fixed),
                  _resident((D_MODEL, IN_COLS), fixed),
                  pl.BlockSpec((1, GATE_COLS), fixed)],
        out_specs=[pl.BlockSpec((tm, SB_WIDTH), row)] * 6
                  + [pl.BlockSpec((tm, GATE_COLS), row)],
        out_shape=[wide(SB_WIDTH, BF16), wide(SB_WIDTH, F32), wide(SB_WIDTH, F32),
                   wide(DF_WIDTH, BF16), wide(DF_WIDTH, F32), wide(DF_WIDTH, F32),
                   wide(GATE_COLS, F32)],
        compiler_params=_params("arbitrary"),
        name="in_proj",
    )(x, g.reshape(1, D_MODEL), w_bf, b_gate.reshape(1, GATE_COLS))


def _in_proj_prompt_body(x_ref, g_ref, w_ref, bg_ref,
                         qsb_ref, ksb_t_ref, vsb_t_ref, ksb_tb_ref, vsb_tb_ref,
                         qdf_ref, kdf_ref, vdf_ref, kdf_b_ref, vdf_b_ref, gate_ref):
    qsb, ksb, vsb, qdf, kdf, vdf, gates = _projections(x_ref, g_ref, w_ref)
    qsb_ref[...] = (qsb() * SB_SCALE).astype(BF16)
    for proj, t_ref, tb_ref in ((ksb, ksb_t_ref, ksb_tb_ref), (vsb, vsb_t_ref, vsb_tb_ref)):
        feature_major = proj().T
        t_ref[0] = feature_major
        tb_ref[0] = feature_major.astype(BF16)
    qdf_ref[...] = (qdf() * DF_SCALE).astype(BF16)
    for proj, h_ref, b_ref in ((kdf, kdf_ref, kdf_b_ref), (vdf, vdf_ref, vdf_b_ref)):
        rows = proj()
        for h in range(DF_HEADS):
            h_ref[0, :, h, :] = rows[:, h * DF_V_DIM:(h + 1) * DF_V_DIM]
        b_ref[...] = rows.astype(BF16)
    gate_ref[...] = jax.nn.sigmoid(gates() + bg_ref[...])


def _in_proj_prompt(x, g, w_bf, b_gate, batch, seq_len):
    n = batch * seq_len
    tm = ROW_TILE
    tiles = seq_len // tm
    row = lambda b, t: (b * tiles + t, 0)
    fixed = lambda b, t: (0, 0)
    feat = pl.BlockSpec((1, SB_WIDTH, tm), lambda b, t: (b, 0, t))
    heads = pl.BlockSpec((1, tm, DF_HEADS, DF_V_DIM), lambda b, t: (b, t, 0, 0))
    rows = pl.BlockSpec((tm, SB_WIDTH), row)
    feat_shape = lambda dt: jax.ShapeDtypeStruct((batch, SB_WIDTH, seq_len), dt)
    head_shape = jax.ShapeDtypeStruct((batch, seq_len, DF_HEADS, DF_V_DIM), F32)
    wide = lambda cols, dt: jax.ShapeDtypeStruct((n, cols), dt)
    return pl.pallas_call(
        _in_proj_prompt_body,
        grid=(batch, tiles),
        in_specs=[pl.BlockSpec((tm, D_MODEL), row),
                  pl.BlockSpec((1, D_MODEL), fixed),
                  _resident((D_MODEL, IN_COLS), fixed),
                  pl.BlockSpec((1, GATE_COLS), fixed)],
        out_specs=[rows, feat, feat, feat, feat, rows, heads, heads, rows, rows,
                   pl.BlockSpec((tm, GATE_COLS), row)],
        out_shape=[wide(SB_WIDTH, BF16), feat_shape(F32), feat_shape(F32),
                   feat_shape(BF16), feat_shape(BF16),
                   wide(DF_WIDTH, BF16), head_shape, head_shape,
                   wide(DF_WIDTH, BF16), wide(DF_WIDTH, BF16),
                   wide(GATE_COLS, F32)],
        compiler_params=_params("arbitrary", "arbitrary"),
        name="in_proj_prompt",
    )(x, g.reshape(1, D_MODEL), w_bf, b_gate.reshape(1, GATE_COLS))


def _prompt_attn_body(slopes_ref, lq1_ref, lk1_ref, lq2_ref, lk2_ref, gn_ref,
                      qsb_ref, ksb_t_ref, vsb_t_ref, qdf_ref, kdf_ref, vdf_ref,
                      ysb_ref, ydf_ref,
                      acc_sb, run_sb, acc_df, m_df, l_df, *, lam_init):
    blk = ATTN_BLOCK
    hd = SB_HEAD_DIM
    grp = pl.program_id(1)
    qi = pl.program_id(2)
    slope = slopes_ref[grp]

    span = ATTN_SPAN
    r = lax.broadcasted_iota(jnp.int32, (blk, span * blk), 0)
    c = lax.broadcasted_iota(jnp.int32, (blk, span * blk), 1)
    strict = (c < r)[:, :blk]
    causal = (c <= r)[:, :blk]
    alibi_local = (-slope) * (r - c).astype(F32)
    u = _suffix_ones(blk)

    acc_sb[...] = jnp.zeros_like(acc_sb)
    run_sb[...] = jnp.zeros_like(run_sb)
    acc_df[...] = jnp.zeros_like(acc_df)
    m_df[...] = jnp.full_like(m_df, NEG_INF)
    l_df[...] = jnp.zeros_like(l_df)

    sb_heads = range(HEAD_GROUP // hd)
    feats = [slice(h * hd, (h + 1) * hd) for h in sb_heads]
    maps = [slice(m * DF_HEAD_DIM, (m + 1) * DF_HEAD_DIM) for m in range(2)]

    def lanes(parts):
        return parts[0] if len(parts) == 1 else jnp.concatenate(parts, axis=-1)

    def sweep(k_first, n_blk, ends_on_diagonal):
        width = n_blk * blk
        keys = pl.ds(pl.multiple_of(k_first * blk, blk), width)
        sub = [slice(i * blk, (i + 1) * blk) for i in range(n_blk)]
        on_diagonal = [ends_on_diagonal and i == n_blk - 1 for i in range(n_blk)]
        sb_valid = [strict if d else None for d in on_diagonal]
        zs = [jnp.dot(qsb_ref[0, :, f], ksb_t_ref[0, f, keys], preferred_element_type=F32)
              for f in feats]
        k2 = kdf_ref[0, keys, :]
        qks = [lax.dot_general(qdf_ref[0, :, c], k2[:, c], _NT, preferred_element_type=F32)
               for c in maps]
        terms = [[_sb_log_terms(z[:, sub[i]], u, sb_valid[i]) for i in range(n_blk)] for z in zs]
        bias = alibi_local[:, :width] - slope * ((qi - k_first) * blk).astype(F32)
        probs = []
        for m in range(2):
            s = qks[m] + bias
            if ends_on_diagonal:
                s = lanes([jnp.where(causal, s[:, sub[i]], NEG_INF) if on_diagonal[i]
                           else s[:, sub[i]] for i in range(n_blk)])
            probs.append(_softmax_probs(s, m_df, l_df, m))
        ws = []
        for h in sb_heads:
            run = run_sb[h]
            w_h = [None] * n_blk
            for i in reversed(range(n_blk)):
                w_h[i] = _sb_weights(terms[h][i], run, sb_valid[i])
                run = run + terms[h][i][2]
            run_sb[h] = run
            ws.append(lanes(w_h))
        v2 = vdf_ref[0, keys, :]
        for m, (alpha, p) in enumerate(probs):
            acc_df[m] = alpha * acc_df[m] + jnp.dot(p, v2, preferred_element_type=F32)
        for h in sb_heads:
            acc_sb[h] += lax.dot_general(ws[h], vsb_t_ref[0, feats[h], keys], _NT,
                                         preferred_element_type=F32)

    n_head = qi % span + 1
    for n in range(1, span + 1):
        @pl.when(n_head == n)
        def _():
            sweep(qi - (n - 1), n, True)

    def older(i, carry):
        sweep(qi - n_head + 1 - (i + 1) * span, span, False)
        return carry

    lax.fori_loop(0, qi // span, older, 0)

    for h in range(HEAD_GROUP // hd):
        ysb_ref[0, :, h * hd:(h + 1) * hd] = acc_sb[h].astype(ysb_ref.dtype)
    lam = _lam(lq1_ref, lk1_ref, lq2_ref, lk2_ref, lam_init)
    o = acc_df[0] / l_df[0] - lam * (acc_df[1] / l_df[1])
    ydf_ref[0] = ((_rms(o) * gn_ref[...]) * (1.0 - lam_init)).astype(ydf_ref.dtype)


def _prompt_attention(slopes, lam_vecs, gn, qsb, ksb_t, vsb_t, qdf, kdf, vdf, lam_init):
    b, t, _ = qsb.shape
    blk = ATTN_BLOCK
    qmap = lambda bi, g, qi: (bi, qi, g)
    fixed = lambda bi, g, qi: (0, 0)
    vec = pl.BlockSpec((1, DF_HEAD_DIM), fixed)
    qspec = pl.BlockSpec((1, blk, HEAD_GROUP), qmap)
    rows = pl.BlockSpec((1, t, HEAD_GROUP), lambda bi, g, qi: (bi, 0, g))
    feat = pl.BlockSpec((1, HEAD_GROUP, t), lambda bi, g, qi: (bi, g, 0))
    return pl.pallas_call(
        functools.partial(_prompt_attn_body, lam_init=lam_init),
        grid=(b, N_GROUPS, t // blk),
        in_specs=[pl.BlockSpec(memory_space=pltpu.SMEM), vec, vec, vec, vec,
                  pl.BlockSpec((1, DF_V_DIM), fixed),
                  qspec, feat, feat, qspec, rows, rows],
        out_specs=[qspec, qspec],
        out_shape=[jax.ShapeDtypeStruct((b, t, SB_WIDTH), BF16),
                   jax.ShapeDtypeStruct((b, t, DF_WIDTH), BF16)],
        scratch_shapes=[pltpu.VMEM((2, blk, SB_HEAD_DIM), F32),
                        pltpu.VMEM((2, blk, 1), F32),
                        pltpu.VMEM((2, blk, DF_V_DIM), F32),
                        pltpu.VMEM((2, blk, 1), F32),
                        pltpu.VMEM((2, blk, 1), F32)],
        compiler_params=_params("arbitrary", "arbitrary", "arbitrary"),
        name="prompt_attention",
    )(slopes, *lam_vecs, gn, qsb, ksb_t, vsb_t, qdf, kdf, vdf)


def _decode_attn_body(pt_ref, slopes_ref, lq1_ref, lk1_ref, lq2_ref, lk2_ref, gn_ref,
                      qsb_ref, ksb_new_ref, vsb_new_ref, qdf_ref, kdf_new_ref, vdf_new_ref,
                      *rest, lam_init, past_len, n_pages, t_new):
    del pt_ref
    npp = PAGES_PER_STEP
    ksb_pgs, vsb_pgs, kdf_pgs, vdf_pgs = (rest[i * npp:(i + 1) * npp] for i in range(4))
    ysb_ref, ydf_ref, qbd_sb, q_df, acc_sb, run_sb, acc_df, m_df, l_df = rest[4 * npp:]
    step = pl.program_id(1)
    rows = SB_HEADS * t_new
    df_rows = 2 * t_new
    pg = PAGE_SIZE

    row_i = lax.broadcasted_iota(jnp.int32, (rows, pg), 0)
    col_i = lax.broadcasted_iota(jnp.int32, (rows, pg), 1)
    tok = row_i % t_new
    u = _suffix_ones(pg)
    df_head = lax.broadcasted_iota(jnp.int32, (rows, 1), 0) // df_rows
    slope = jnp.zeros((rows, 1), F32)
    for h in range(DF_HEADS):
        slope = jnp.where(df_head == h, slopes_ref[h], slope)


    @pl.when(step == 0)
    def _():
        r = lax.broadcasted_iota(jnp.int32, (rows, SB_WIDTH), 0) // t_new
        cb = lax.broadcasted_iota(jnp.int32, (rows, SB_WIDTH), 1) // SB_HEAD_DIM
        q = qsb_ref[0].astype(F32)
        qbd_sb[...] = jnp.where(r == cb, jnp.concatenate([q] * SB_HEADS, axis=0),
                                0.0).astype(BF16)
        q = qdf_ref[0].astype(F32)
        first_half = lax.broadcasted_iota(jnp.int32, (t_new, DF_QK_DIM), 1) < DF_HEAD_DIM
        parts = []
        for h in range(DF_HEADS):
            qh = q[:, h * DF_QK_DIM:(h + 1) * DF_QK_DIM]
            parts += [jnp.where(first_half, qh, 0.0), jnp.where(first_half, 0.0, qh)]
        q_df[...] = jnp.concatenate(parts, axis=0).astype(BF16)
        acc_df[...] = jnp.zeros_like(acc_df)
        m_df[...] = jnp.full_like(m_df, NEG_INF)
        l_df[...] = jnp.zeros_like(l_df)

        def padded(ref):
            pad = jnp.zeros((pg - t_new, ref.shape[-1]), F32)
            return jnp.concatenate([ref[0], pad], axis=0).astype(BF16)

        ksb, vsb, kdf, vdf = (padded(ksb_new_ref), padded(vsb_new_ref),
                              padded(kdf_new_ref), padded(vdf_new_ref))
        z = lax.dot_general(qbd_sb[...], ksb, _NT, preferred_element_type=F32)
        terms = _sb_log_terms(z, u, col_i < tok)
        acc_sb[...] = jnp.dot(_sb_weights(terms, 0.0, col_i < tok), vsb,
                              preferred_element_type=F32)
        run_sb[...] = terms[2]
        head = lambda a, h: a[:, h * DF_V_DIM:(h + 1) * DF_V_DIM]
        s = jnp.concatenate(
            [lax.dot_general(q_df[h * df_rows:(h + 1) * df_rows, :], head(kdf, h), _NT,
                             preferred_element_type=F32) for h in range(DF_HEADS)], axis=0)
        s = jnp.where(col_i <= tok, s - slope * (tok - col_i).astype(F32), NEG_INF)
        alpha, p = _softmax_probs(s, m_df, l_df, slice(None))
        acc_df[...] = alpha * acc_df[...] + jnp.concatenate(
            [jnp.dot(p[h * df_rows:(h + 1) * df_rows, :], head(vdf, h),
                     preferred_element_type=F32) for h in range(DF_HEADS)], axis=0)

    first_page = n_pages - 1 - step * npp
    zs = [jnp.dot(qbd_sb[...], ksb_pgs[i][0, 0].astype(BF16), preferred_element_type=F32)
          for i in range(npp)]
    qks = [lax.dot_general(q_df[...], kdf_pgs[i][0, 0].astype(BF16), _NT,
                           preferred_element_type=F32) for i in range(npp)]
    terms = [_sb_log_terms(z, u, None) for z in zs]

    wide = DF_HEADS * pg
    r_w = lax.broadcasted_iota(jnp.int32, (rows, wide), 0)
    c_w = lax.broadcasted_iota(jnp.int32, (rows, wide), 1)
    own_head = (c_w % DF_HEADS) == (r_w // df_rows)
    local = jnp.where(own_head, (-slope) * (r_w % t_new - c_w // DF_HEADS).astype(F32), NEG_INF)
    scores = []
    for i in range(npp):
        page_start = (first_page - i) * pg
        scores.append(qks[i] + (local - slope * (past_len - page_start).astype(F32)))
    alpha, p = _softmax_probs(jnp.concatenate(scores, axis=-1), m_df, l_df, slice(None))

    run = run_sb[...]
    ws = []
    for i in range(npp):
        ws.append(_sb_weights(terms[i], run, None))
        run = run + terms[i][2]
    run_sb[...] = run

    acc_df[...] = alpha * acc_df[...] + sum(
        jnp.dot(p[:, i * wide:(i + 1) * wide], vdf_pgs[i][0, 0].astype(BF16),
                preferred_element_type=F32) for i in range(npp))
    acc_sb[...] += sum(lax.dot_general(ws[i], vsb_pgs[i][0, 0].astype(BF16), _NT,
                                       preferred_element_type=F32) for i in range(npp))

    @pl.when(step == pl.num_programs(1) - 1)
    def _():
        r = lax.broadcasted_iota(jnp.int32, (rows, SB_WIDTH), 0) // t_new
        cb = lax.broadcasted_iota(jnp.int32, (rows, SB_WIDTH), 1) // SB_HEAD_DIM
        ysb = jnp.where(r == cb, acc_sb[...], 0.0).reshape(SB_HEADS, t_new, SB_WIDTH).sum(axis=0)
        ysb_ref[0] = ysb.astype(ysb_ref.dtype)

        lam = _lam(lq1_ref, lk1_ref, lq2_ref, lk2_ref, lam_init)
        normed = acc_df[...] / l_df[...]
        outs = []
        for h in range(DF_HEADS):
            first = normed[h * df_rows:h * df_rows + t_new, :]
            second = normed[h * df_rows + t_new:(h + 1) * df_rows, :]
            o = first - lam * second
            outs.append((_rms(o) * gn_ref[...]) * (1.0 - lam_init))
        ydf_ref[0] = jnp.concatenate(outs, axis=-1).astype(ydf_ref.dtype)


def _decode_attention(page_table, slopes, lam_vecs, gn, qsb, ksb, vsb, qdf, kdf, vdf,
                      cache_sb_k_t, cache_sb_v_t, cache_df_k, cache_df_v, layer, lam_init):
    nb, t_new, _ = qsb.shape
    n_pages = page_table.shape[1]
    past_len = n_pages * PAGE_SIZE
    rows = SB_HEADS * t_new
    new = pl.BlockSpec((1, t_new, SB_WIDTH), lambda bi, s, pt: (bi, 0, 0))
    fixed = lambda bi, s, pt: (0, 0)
    vec = pl.BlockSpec((1, DF_HEAD_DIM), fixed)

    npp = PAGES_PER_STEP

    def page_spec(shape, i):
        return pl.BlockSpec(
            (1, 1) + shape,
            lambda bi, s, pt: (layer, pt[bi, n_pages - 1 - (s * npp + i)], 0, 0))

    sb_pages = [page_spec((SB_WIDTH, PAGE_SIZE), i) for i in range(npp)]
    df_pages = [page_spec((PAGE_SIZE * DF_HEADS, DF_V_DIM), i) for i in range(npp)]
    grid_spec = pltpu.PrefetchScalarGridSpec(
        num_scalar_prefetch=1,
        grid=(nb, n_pages // npp),
        in_specs=[pl.BlockSpec(memory_space=pltpu.SMEM), vec, vec, vec, vec,
                  pl.BlockSpec((1, DF_V_DIM), fixed),
                  new, new, new, new, new, new] + sb_pages + sb_pages + df_pages + df_pages,
        out_specs=[new, new],
        scratch_shapes=[pltpu.VMEM((rows, SB_WIDTH), BF16),
                        pltpu.VMEM((rows, DF_QK_DIM), BF16),
                        pltpu.VMEM((rows, SB_WIDTH), F32),
                        pltpu.VMEM((rows, 1), F32),
                        pltpu.VMEM((rows, DF_V_DIM), F32),
                        pltpu.VMEM((rows, 1), F32),
                        pltpu.VMEM((rows, 1), F32)],
    )
    return pl.pallas_call(
        functools.partial(_decode_attn_body, lam_init=lam_init, past_len=past_len,
                          n_pages=n_pages, t_new=t_new),
        grid_spec=grid_spec,
        out_shape=[jax.ShapeDtypeStruct((nb, t_new, SB_WIDTH), BF16),
                   jax.ShapeDtypeStruct((nb, t_new, DF_WIDTH), BF16)],
        compiler_params=_params("arbitrary", "arbitrary"),
        name="decode_attention",
    )(page_table, slopes, *lam_vecs, gn, qsb, ksb, vsb, qdf, kdf, vdf,
      *([cache_sb_k_t] * npp), *([cache_sb_v_t] * npp),
      *([cache_df_k] * npp), *([cache_df_v] * npp))


def _merge_out_body(x_ref, ysb_ref, ydf_ref, gate_ref, wsb_ref, wdf_ref, wout_ref, o_ref):
    br_sb = jnp.dot(ysb_ref[...], wsb_ref[...], preferred_element_type=F32)
    br_df = jnp.dot(ydf_ref[...], wdf_ref[...], preferred_element_type=F32)
    merged = gate_ref[:, :D_MODEL] * br_sb + gate_ref[:, D_MODEL:] * br_df
    o_ref[...] = x_ref[...] + jnp.dot(merged.astype(BF16), wout_ref[...],
                                      preferred_element_type=F32)


def _merge_out(x, ysb, ydf, gates, wsb_bf, wdf_bf, wout_bf):
    n = x.shape[0]
    tm = min(ROW_TILE, n)
    row = lambda i: (i, 0)
    fixed = lambda i: (0, 0)
    return pl.pallas_call(
        _merge_out_body,
        grid=(n // tm,),
        in_specs=[pl.BlockSpec((tm, D_MODEL), row),
                  pl.BlockSpec((tm, SB_WIDTH), row),
                  pl.BlockSpec((tm, DF_WIDTH), row),
                  pl.BlockSpec((tm, GATE_COLS), row),
                  _resident((SB_WIDTH, D_MODEL), fixed),
                  _resident((DF_WIDTH, D_MODEL), fixed),
                  _resident((D_MODEL, D_MODEL), fixed)],
        out_specs=pl.BlockSpec((tm, D_MODEL), row),
        out_shape=jax.ShapeDtypeStruct((n, D_MODEL), F32),
        compiler_params=_params("arbitrary"),
        name="merge_out",
    )(x, ysb, ydf, gates, wsb_bf, wdf_bf, wout_bf)


def _conv_ffn_body(*refs, seq_len, tile_rows, has_state):
    if has_state:
        (x_ref, g_ref, wup_ref, cw_ref, cb_ref, wdn_ref, e1_ref, e2_ref,
         o_ref, tail_ref) = refs
    else:
        (x_ref, g_ref, wup_ref, cw_ref, cb_ref, wdn_ref,
         o_ref, tail_ref, carry) = refs

        @pl.when(pl.program_id(1) == 0)
        def _():
            carry[...] = jnp.zeros_like(carry)

    x = x_ref[...]
    hb = (_rms(x) * g_ref[...]).astype(BF16)
    row = lax.broadcasted_iota(jnp.int32, (tile_rows, FF_CHUNK), 0)
    t_in_seq = row % seq_len if has_state else row
    acc = x
    for ci in range(D_FF // FF_CHUNK):
        cols = slice(ci * FF_CHUNK, (ci + 1) * FF_CHUNK)
        gcols = slice(D_FF + ci * FF_CHUNK, D_FF + (ci + 1) * FF_CHUNK)
        a = jnp.dot(hb, wup_ref[:, cols], preferred_element_type=F32)
        b = jnp.dot(hb, wup_ref[:, gcols], preferred_element_type=F32)
        a1 = pltpu.roll(a, 1, 0)
        a2 = pltpu.roll(a, 2, 0)
        if has_state:
            a1 = jnp.where(t_in_seq == 0, e1_ref[:, cols], a1)
            a2 = jnp.where(t_in_seq < 2, e2_ref[:, cols], a2)
            tail_ref[:, cols] = a
        else:
            last = carry[SUBLANES - 1:SUBLANES, cols]
            prev = carry[SUBLANES - 2:SUBLANES - 1, cols]
            a1 = jnp.where(t_in_seq == 0, last, a1)
            a2 = jnp.where(t_in_seq == 0, prev, jnp.where(t_in_seq == 1, last, a2))
            tail = a[tile_rows - SUBLANES:, :]
            carry[:, cols] = tail
            tail_ref[0, :, cols] = tail
        conv = cb_ref[:, cols] + cw_ref[0:1, cols] * a2
        conv = conv + cw_ref[1:2, cols] * a1
        conv = conv + cw_ref[2:3, cols] * a
        gated = (jax.nn.gelu(conv) * b).astype(BF16)
        acc = acc + jnp.dot(gated, wdn_ref[cols, :], preferred_element_type=F32)
    o_ref[...] = acc


def _conv_ffn(x, g, wup_bf, conv_w, conv_b, wdn_bf, batch, seq_len, state):
    n = x.shape[0]
    has_state = state is not None
    tm = STATE_ROW_TILE if has_state else ROW_TILE
    body = functools.partial(_conv_ffn_body, seq_len=seq_len, has_state=has_state,
                             tile_rows=tm)
    weights = [g.reshape(1, D_MODEL), wup_bf, conv_w, conv_b.reshape(1, D_FF), wdn_bf]
    if has_state:
        pad = lambda s: jnp.pad(s, ((0, 0), (0, seq_len - s.shape[1]), (0, 0))).reshape(n, D_FF)
        e1 = pad(state[:, 1:2])
        e2 = pad(state)
        row = lambda i: (i, 0)
        fixed = lambda i: (0, 0)
        out, gate_rows = pl.pallas_call(
            body,
            grid=(n // tm,),
            in_specs=[pl.BlockSpec((tm, D_MODEL), row),
                      pl.BlockSpec((1, D_MODEL), fixed),
                      _resident((D_MODEL, 2 * D_FF), fixed),
                      pl.BlockSpec((CONV_WIDTH, D_FF), fixed),
                      pl.BlockSpec((1, D_FF), fixed),
                      _resident((D_FF, D_MODEL), fixed),
                      pl.BlockSpec((tm, D_FF), row),
                      pl.BlockSpec((tm, D_FF), row)],
            out_specs=[pl.BlockSpec((tm, D_MODEL), row), pl.BlockSpec((tm, D_FF), row)],
            out_shape=[jax.ShapeDtypeStruct((n, D_MODEL), F32),
                       jax.ShapeDtypeStruct((n, D_FF), F32)],
            compiler_params=_params("arbitrary"),
            name="conv_ffn_decode",
        )(x, *weights, e1, e2)
        tail = gate_rows.reshape(batch, seq_len, D_FF)[:, seq_len - (CONV_WIDTH - 1):]
        return out, tail
    tiles = seq_len // tm
    row = lambda bi, ti: (bi * tiles + ti, 0)
    fixed = lambda bi, ti: (0, 0)
    out, tail8 = pl.pallas_call(
        body,
        grid=(batch, tiles),
        in_specs=[pl.BlockSpec((tm, D_MODEL), row),
                  pl.BlockSpec((1, D_MODEL), fixed),
                  _resident((D_MODEL, 2 * D_FF), fixed),
                  pl.BlockSpec((CONV_WIDTH, D_FF), fixed),
                  pl.BlockSpec((1, D_FF), fixed),
                  _resident((D_FF, D_MODEL), fixed)],
        out_specs=[pl.BlockSpec((tm, D_MODEL), row),
                   pl.BlockSpec((1, SUBLANES, D_FF), lambda bi, ti: (bi, 0, 0))],
        out_shape=[jax.ShapeDtypeStruct((n, D_MODEL), F32),
                   jax.ShapeDtypeStruct((batch, SUBLANES, D_FF), F32)],
        scratch_shapes=[pltpu.VMEM((SUBLANES, D_FF), F32)],
        compiler_params=_params("arbitrary", "arbitrary"),
        name="conv_ffn_prompt",
    )(x, *weights)
    return out, tail8[:, SUBLANES - (CONV_WIDTH - 1):]


def _final_norm_body(x_ref, g_ref, o_ref):
    o_ref[...] = _rms(x_ref[...]) * g_ref[...]


def _final_norm(x, g):
    n = x.shape[0]
    tm = min(ROW_TILE, n)
    return pl.pallas_call(
        _final_norm_body,
        grid=(n // tm,),
        in_specs=[pl.BlockSpec((tm, D_MODEL), lambda i: (i, 0)),
                  pl.BlockSpec((1, D_MODEL), lambda i: (0, 0))],
        out_specs=pl.BlockSpec((tm, D_MODEL), lambda i: (i, 0)),
        out_shape=jax.ShapeDtypeStruct((n, D_MODEL), F32),
        compiler_params=_params("arbitrary"),
        name="final_norm",
    )(x, g.reshape(1, D_MODEL))


def _layer_consts(wts, l):
    lam_init = 0.8 - 0.6 * math.exp(-0.3 * l)
    lam_vecs = [wts[k][l].reshape(1, DF_HEAD_DIM)
                for k in ("df_lam_q1", "df_lam_k1", "df_lam_q2", "df_lam_k2")]
    return lam_init, lam_vecs, wts["df_norm_g"][l].reshape(1, DF_V_DIM)


def _mix_and_ffn(xf, ysb, ydf, gates, wts, l, batch, seq_len, state):
    n = xf.shape[0]
    xf = _merge_out(xf, ysb.reshape(n, SB_WIDTH), ydf.reshape(n, DF_WIDTH), gates,
                    wts["w_br_sb"][l], wts["w_br_df"][l], wts["w_out"][l])
    return _conv_ffn(xf, wts["ffn_norm_g"][l], wts["w_up"][l], wts["conv_w"][l],
                     wts["conv_b"][l], wts["w_down"][l], batch, seq_len, state)


def _prompt_trunk(x, slopes, wts):
    batch, seq_len, _ = x.shape
    xf = x.reshape(batch * seq_len, D_MODEL)
    rows = ([], [], [], [])
    convs = []
    for l in range(DEPTH):
        lam_init, lam_vecs, gn = _layer_consts(wts, l)
        (qsb, ksb_t, vsb_t, ksb_tb, vsb_tb, qdf, kdf, vdf, kdf_b, vdf_b, gates) = _in_proj_prompt(
            xf, wts["attn_norm_g"][l], wts["w_in"][l], wts["b_gate"][l], batch, seq_len)
        tok = lambda a: a.reshape(batch, seq_len, SB_WIDTH)
        ysb, ydf = _prompt_attention(slopes, lam_vecs, gn, tok(qsb), ksb_tb, vsb_tb,
                                     tok(qdf), tok(kdf_b), tok(vdf_b), lam_init)
        xf, tail = _mix_and_ffn(xf, ysb, ydf, gates, wts, l, batch, seq_len, None)
        for lst, a in zip(rows, (ksb_t, vsb_t, kdf, vdf)):
            lst.append(a)
        convs.append(tail)
    y = _final_norm(xf, wts["final_norm_g"]).reshape(batch, seq_len, D_MODEL)
    to_tokens = lambda a: a.reshape(DEPTH, batch, SB_HEADS, SB_HEAD_DIM, seq_len).transpose(0, 1, 4, 2, 3)
    return (y, to_tokens(jnp.stack(rows[0])), to_tokens(jnp.stack(rows[1])),
            jnp.stack(rows[2]), jnp.stack(rows[3]), jnp.stack(convs))


def _decode_trunk(x, caches, conv_state, page_table, slopes, wts):
    batch, seq_len, _ = x.shape
    n = batch * seq_len
    xf = x.reshape(n, D_MODEL)
    depth, n_phys = caches[0].shape[:2]
    sb_t = lambda c: c.transpose(0, 1, 3, 4, 2).reshape(depth, n_phys, SB_WIDTH, PAGE_SIZE)
    df_rows = lambda c: c.reshape(depth, n_phys, PAGE_SIZE * DF_HEADS, DF_V_DIM)
    caches = (sb_t(caches[0]), sb_t(caches[1]), df_rows(caches[2]), df_rows(caches[3]))
    rows = ([], [], [], [])
    convs = []
    for l in range(DEPTH):
        lam_init, lam_vecs, gn = _layer_consts(wts, l)
        qsb, ksb, vsb, qdf, kdf, vdf, gates = _in_proj(
            xf, wts["attn_norm_g"][l], wts["w_in"][l], wts["b_gate"][l])
        tok = lambda a: a.reshape(batch, seq_len, SB_WIDTH)
        ysb, ydf = _decode_attention(page_table, slopes, lam_vecs, gn, tok(qsb), tok(ksb),
                                     tok(vsb), tok(qdf), tok(kdf), tok(vdf), *caches, l, lam_init)
        xf, tail = _mix_and_ffn(xf, ysb, ydf, gates, wts, l, batch, seq_len, conv_state[l])
        rows[0].append(ksb.reshape(batch, seq_len, SB_HEADS, SB_HEAD_DIM))
        rows[1].append(vsb.reshape(batch, seq_len, SB_HEADS, SB_HEAD_DIM))
        rows[2].append(kdf.reshape(batch, seq_len, DF_HEADS, DF_QK_DIM))
        rows[3].append(vdf.reshape(batch, seq_len, DF_HEADS, DF_V_DIM))
        convs.append(tail)
    y = _final_norm(xf, wts["final_norm_g"]).reshape(batch, seq_len, D_MODEL)
    return (y, *[jnp.stack(r) for r in rows], jnp.stack(convs))


def kernel(x_prompt, x_sample, cache_sb_k, cache_sb_v, cache_df_k, cache_df_v, state_ffn_conv,
           page_table, attn_norm_g, w_in, b_gate, df_lam_q1, df_lam_k1, df_lam_q2, df_lam_k2,
           df_norm_g, w_br_sb, w_br_df, w_out, ffn_norm_g, w_up, conv_w, conv_b, w_down,
           final_norm_g):
    wts = dict(attn_norm_g=attn_norm_g, w_in=w_in.astype(BF16), b_gate=b_gate,
               df_lam_q1=df_lam_q1, df_lam_k1=df_lam_k1, df_lam_q2=df_lam_q2,
               df_lam_k2=df_lam_k2, df_norm_g=df_norm_g,
               w_br_sb=w_br_sb.astype(BF16), w_br_df=w_br_df.astype(BF16),
               w_out=w_out.astype(BF16), ffn_norm_g=ffn_norm_g, w_up=w_up.astype(BF16),
               conv_w=conv_w, conv_b=conv_b, w_down=w_down.astype(BF16),
               final_norm_g=final_norm_g)
    slopes = 2.0 ** (-8.0 * jnp.arange(1, DF_HEADS + 1, dtype=F32) / DF_HEADS)
    caches = (cache_sb_k, cache_sb_v, cache_df_k, cache_df_v)
    y_p, *rest_p = _prompt_trunk(x_prompt, slopes, wts)
    y_s, *rest_s = _decode_trunk(x_sample, caches, state_ffn_conv, page_table, slopes, wts)
    return (y_p, y_s, *rest_p, *rest_s)
```

```python
import functools
import math

import jax
import jax.numpy as jnp
from jax import lax
from jax.experimental import pallas as pl
from jax.experimental.pallas import tpu as pltpu

D_MODEL = 1024
DEPTH = 4
PAGE_SIZE = 128
SB_HEADS = 8
SB_HEAD_DIM = 64
SB_WIDTH = SB_HEADS * SB_HEAD_DIM
DF_HEADS = 4
DF_HEAD_DIM = 64
DF_QK_DIM = 2 * DF_HEAD_DIM
DF_V_DIM = 2 * DF_HEAD_DIM
DF_WIDTH = DF_HEADS * DF_V_DIM
GATE_COLS = 2 * D_MODEL
IN_COLS = 3 * SB_WIDTH + 2 * DF_HEADS * DF_QK_DIM + DF_WIDTH + GATE_COLS
D_FF = 2816
CONV_WIDTH = 3
RMS_EPS = 1e-6
NEG_INF = -1e30
SB_SCALE = SB_HEAD_DIM ** -0.5
DF_SCALE = DF_HEAD_DIM ** -0.5

LANES = 128
SUBLANES = 8
MXU_DIM = 256
VMEM_LIMIT_BYTES = 56 * 1024 * 1024

ROW_TILE = 512
STATE_ROW_TILE = 256
ATTN_BLOCK = 256
ATTN_SPAN = 4
FF_CHUNK = MXU_DIM
PAGES_PER_STEP = 16
HEAD_GROUP = LANES
N_GROUPS = SB_WIDTH // HEAD_GROUP

F32 = jnp.float32
BF16 = jnp.bfloat16

_NT = (((1,), (1,)), ((), ()))


def _params(*semantics):
    return pltpu.CompilerParams(dimension_semantics=semantics,
                                vmem_limit_bytes=VMEM_LIMIT_BYTES)


def _resident(shape, index_map):
    return pl.BlockSpec(shape, index_map, pipeline_mode=pl.Buffered(1))


def _rms(x):
    return x * lax.rsqrt(jnp.mean(x * x, axis=-1, keepdims=True) + RMS_EPS)


def _softplus(z):
    return jnp.maximum(z, 0.0) + jnp.log(1.0 + jnp.exp(-jnp.abs(z)))


def _suffix_ones(n):
    s = lax.broadcasted_iota(jnp.int32, (n, n), 0)
    j = lax.broadcasted_iota(jnp.int32, (n, n), 1)
    return jnp.where(s > j, 1.0, 0.0).astype(BF16)


def _suffix_sum(x, u):
    hi = x.astype(BF16)
    lo = (x - hi.astype(F32)).astype(BF16)
    return (jnp.dot(hi, u, preferred_element_type=F32)
            + jnp.dot(lo, u, preferred_element_type=F32))


def _lam(lq1_ref, lk1_ref, lq2_ref, lk2_ref, lam_init):
    s1 = jnp.sum(lq1_ref[...] * lk1_ref[...], axis=-1, keepdims=True)
    s2 = jnp.sum(lq2_ref[...] * lk2_ref[...], axis=-1, keepdims=True)
    return jnp.exp(s1) - jnp.exp(s2) + lam_init


def _sb_log_terms(z, u, valid):
    sp = _softplus(z)
    log_keep = -sp if valid is None else jnp.where(valid, -sp, 0.0)
    return z - sp, _suffix_sum(log_keep, u), jnp.sum(log_keep, axis=-1, keepdims=True)


def _sb_weights(terms, run, valid):
    log_hit, later, _ = terms
    w = jnp.exp(log_hit + (later + run))
    if valid is not None:
        w = jnp.where(valid, w, 0.0)
    return w.astype(BF16)


def _softmax_probs(s, m_ref, l_ref, idx):
    m_old = m_ref[idx]
    m_new = jnp.maximum(m_old, jnp.max(s, axis=-1, keepdims=True))
    alpha = jnp.exp(m_old - m_new)
    p = jnp.exp(s - m_new)
    l_ref[idx] = alpha * l_ref[idx] + jnp.sum(p, axis=-1, keepdims=True)
    m_ref[idx] = m_new
    return alpha, p.astype(BF16)


def _projections(x_ref, g_ref, w_ref):
    hb = (_rms(x_ref[...]) * g_ref[...]).astype(BF16)
    bounds = [0]
    for width in (SB_WIDTH, SB_WIDTH, SB_WIDTH, DF_WIDTH, DF_WIDTH, DF_WIDTH, GATE_COLS):
        bounds.append(bounds[-1] + width)
    return [functools.partial(
        lambda lo, hi: jnp.dot(hb, w_ref[:, lo:hi], preferred_element_type=F32), lo, hi)
        for lo, hi in zip(bounds[:-1], bounds[1:])]


def _in_proj_body(x_ref, g_ref, w_ref, bg_ref,
                  qsb_ref, ksb_ref, vsb_ref, qdf_ref, kdf_ref, vdf_ref, gate_ref):
    qsb, ksb, vsb, qdf, kdf, vdf, gates = _projections(x_ref, g_ref, w_ref)
    qsb_ref[...] = (qsb() * SB_SCALE).astype(BF16)
    ksb_ref[...] = ksb()
    vsb_ref[...] = vsb()
    qdf_ref[...] = (qdf() * DF_SCALE).astype(BF16)
    kdf_ref[...] = kdf()
    vdf_ref[...] = vdf()
    gate_ref[...] = jax.nn.sigmoid(gates() + bg_ref[...])


def _in_proj(x, g, w_bf, b_gate):
    n = x.shape[0]
    tm = min(ROW_TILE, n)
    row = lambda i: (i, 0)
    fixed = lambda i: (0, 0)
    wide = lambda cols, dt: jax.ShapeDtypeStruct((n, cols), dt)
    return pl.pallas_call(
        _in_proj_body,
        grid=(n // tm,),
        in_specs=[pl.BlockSpec((tm, D_MODEL), row),
                  pl.BlockSpec((1, D_MODEL), fixed),
                  _resident((D_MODEL, IN_COLS), fixed),
                  pl.BlockSpec((1, GATE_COLS), fixed)],
        out_specs=[pl.BlockSpec((tm, SB_WIDTH), row)] * 6
                  + [pl.BlockSpec((tm, GATE_COLS), row)],
        out_shape=[wide(SB_WIDTH, BF16), wide(SB_WIDTH, F32), wide(SB_WIDTH, F32),
                   wide(DF_WIDTH, BF16), wide(DF_WIDTH, F32), wide(DF_WIDTH, F32),
                   wide(GATE_COLS, F32)],
        compiler_params=_params("arbitrary"),
        name="in_proj",
    )(x, g.reshape(1, D_MODEL), w_bf, b_gate.reshape(1, GATE_COLS))


def _in_proj_prompt_body(x_ref, g_ref, w_ref, bg_ref,
                         qsb_ref, ksb_t_ref, vsb_t_ref, ksb_tb_ref, vsb_tb_ref,
                         qdf_ref, kdf_ref, vdf_ref, kdf_b_ref, vdf_b_ref, gate_ref):
    qsb, ksb, vsb, qdf, kdf, vdf, gates = _projections(x_ref, g_ref, w_ref)
    qsb_ref[...] = (qsb() * SB_SCALE).astype(BF16)
    for proj, t_ref, tb_ref in ((ksb, ksb_t_ref, ksb_tb_ref), (vsb, vsb_t_ref, vsb_tb_ref)):
        feature_major = proj().T
        t_ref[0] = feature_major
        tb_ref[0] = feature_major.astype(BF16)
    qdf_ref[...] = (qdf() * DF_SCALE).astype(BF16)
    for proj, h_ref, b_ref in ((kdf, kdf_ref, kdf_b_ref), (vdf, vdf_ref, vdf_b_ref)):
        rows = proj()
        for h in range(DF_HEADS):
            h_ref[0, :, h, :] = rows[:, h * DF_V_DIM:(h + 1) * DF_V_DIM]
        b_ref[...] = rows.astype(BF16)
    gate_ref[...] = jax.nn.sigmoid(gates() + bg_ref[...])


def _in_proj_prompt(x, g, w_bf, b_gate, batch, seq_len):
    n = batch * seq_len
    tm = ROW_TILE
    tiles = seq_len // tm
    row = lambda b, t: (b * tiles + t, 0)
    fixed = lambda b, t: (0, 0)
    feat = pl.BlockSpec((1, SB_WIDTH, tm), lambda b, t: (b, 0, t))
    heads = pl.BlockSpec((1, tm, DF_HEADS, DF_V_DIM), lambda b, t: (b, t, 0, 0))
    rows = pl.BlockSpec((tm, SB_WIDTH), row)
    feat_shape = lambda dt: jax.ShapeDtypeStruct((batch, SB_WIDTH, seq_len), dt)
    head_shape = jax.ShapeDtypeStruct((batch, seq_len, DF_HEADS, DF_V_DIM), F32)
    wide = lambda cols, dt: jax.ShapeDtypeStruct((n, cols), dt)
    return pl.pallas_call(
        _in_proj_prompt_body,
        grid=(batch, tiles),
        in_specs=[pl.BlockSpec((tm, D_MODEL), row),
                  pl.BlockSpec((1, D_MODEL), fixed),
                  _resident((D_MODEL, IN_COLS), fixed),
                  pl.BlockSpec((1, GATE_COLS), fixed)],
        out_specs=[rows, feat, feat, feat, feat, rows, heads, heads, rows, rows,
                   pl.BlockSpec((tm, GATE_COLS), row)],
        out_shape=[wide(SB_WIDTH, BF16), feat_shape(F32), feat_shape(F32),
                   feat_shape(BF16), feat_shape(BF16),
                   wide(DF_WIDTH, BF16), head_shape, head_shape,
                   wide(DF_WIDTH, BF16), wide(DF_WIDTH, BF16),
                   wide(GATE_COLS, F32)],
        compiler_params=_params("arbitrary", "arbitrary"),
        name="in_proj_prompt",
    )(x, g.reshape(1, D_MODEL), w_bf, b_gate.reshape(1, GATE_COLS))


def _prompt_attn_body(slopes_ref, lq1_ref, lk1_ref, lq2_ref, lk2_ref, gn_ref,
                      qsb_ref, ksb_t_ref, vsb_t_ref, qdf_ref, kdf_ref, vdf_ref,
                      ysb_ref, ydf_ref,
                      acc_sb, run_sb, acc_df, m_df, l_df, *, lam_init):
    blk = ATTN_BLOCK
    hd = SB_HEAD_DIM
    grp = pl.program_id(1)
    qi = pl.program_id(2)
    slope = slopes_ref[grp]

    span = ATTN_SPAN
    r = lax.broadcasted_iota(jnp.int32, (blk, span * blk), 0)
    c = lax.broadcasted_iota(jnp.int32, (blk, span * blk), 1)
    strict = (c < r)[:, :blk]
    causal = (c <= r)[:, :blk]
    alibi_local = (-slope) * (r - c).astype(F32)
    u = _suffix_ones(blk)

    acc_sb[...] = jnp.zeros_like(acc_sb)
    run_sb[...] = jnp.zeros_like(run_sb)
    acc_df[...] = jnp.zeros_like(acc_df)
    m_df[...] = jnp.full_like(m_df, NEG_INF)
    l_df[...] = jnp.zeros_like(l_df)

    sb_heads = range(HEAD_GROUP // hd)
    feats = [slice(h * hd, (h + 1) * hd) for h in sb_heads]
    maps = [slice(m * DF_HEAD_DIM, (m + 1) * DF_HEAD_DIM) for m in range(2)]

    def lanes(parts):
        return parts[0] if len(parts) == 1 else jnp.concatenate(parts, axis=-1)

    def sweep(k_first, n_blk, ends_on_diagonal):
        width = n_blk * blk
        keys = pl.ds(pl.multiple_of(k_first * blk, blk), width)
        sub = [slice(i * blk, (i + 1) * blk) for i in range(n_blk)]
        on_diagonal = [ends_on_diagonal and i == n_blk - 1 for i in range(n_blk)]
        sb_valid = [strict if d else None for d in on_diagonal]
        zs = [jnp.dot(qsb_ref[0, :, f], ksb_t_ref[0, f, keys], preferred_element_type=F32)
              for f in feats]
        k2 = kdf_ref[0, keys, :]
        qks = [lax.dot_general(qdf_ref[0, :, c], k2[:, c], _NT, preferred_element_type=F32)
               for c in maps]
        terms = [[_sb_log_terms(z[:, sub[i]], u, sb_valid[i]) for i in range(n_blk)] for z in zs]
        bias = alibi_local[:, :width] - slope * ((qi - k_first) * blk).astype(F32)
        probs = []
        for m in range(2):
            s = qks[m] + bias
            if ends_on_diagonal:
                s = lanes([jnp.where(causal, s[:, sub[i]], NEG_INF) if on_diagonal[i]
                           else s[:, sub[i]] for i in range(n_blk)])
            probs.append(_softmax_probs(s, m_df, l_df, m))
        ws = []
        for h in sb_heads:
            run = run_sb[h]
            w_h = [None] * n_blk
            for i in reversed(range(n_blk)):
                w_h[i] = _sb_weights(terms[h][i], run, sb_valid[i])
                run = run + terms[h][i][2]
            run_sb[h] = run
            ws.append(lanes(w_h))
        v2 = vdf_ref[0, keys, :]
        for m, (alpha, p) in enumerate(probs):
            acc_df[m] = alpha * acc_df[m] + jnp.dot(p, v2, preferred_element_type=F32)
        for h in sb_heads:
            acc_sb[h] += lax.dot_general(ws[h], vsb_t_ref[0, feats[h], keys], _NT,
                                         preferred_element_type=F32)

    n_head = qi % span + 1
    for n in range(1, span + 1):
        @pl.when(n_head == n)
        def _():
            sweep(qi - (n - 1), n, True)

    def older(i, carry):
        sweep(qi - n_head + 1 - (i + 1) * span, span, False)
        return carry

    lax.fori_loop(0, qi // span, older, 0)

    for h in range(HEAD_GROUP // hd):
        ysb_ref[0, :, h * hd:(h + 1) * hd] = acc_sb[h].astype(ysb_ref.dtype)
    lam = _lam(lq1_ref, lk1_ref, lq2_ref, lk2_ref, lam_init)
    o = acc_df[0] / l_df[0] - lam * (acc_df[1] / l_df[1])
    ydf_ref[0] = ((_rms(o) * gn_ref[...]) * (1.0 - lam_init)).astype(ydf_ref.dtype)


def _prompt_attention(slopes, lam_vecs, gn, qsb, ksb_t, vsb_t, qdf, kdf, vdf, lam_init):
    b, t, _ = qsb.shape
    blk = ATTN_BLOCK
    qmap = lambda bi, g, qi: (bi, qi, g)
    fixed = lambda bi, g, qi: (0, 0)
    vec = pl.BlockSpec((1, DF_HEAD_DIM), fixed)
    qspec = pl.BlockSpec((1, blk, HEAD_GROUP), qmap)
    rows = pl.BlockSpec((1, t, HEAD_GROUP), lambda bi, g, qi: (bi, 0, g))
    feat = pl.BlockSpec((1, HEAD_GROUP, t), lambda bi, g, qi: (bi, g, 0))
    return pl.pallas_call(
        functools.partial(_prompt_attn_body, lam_init=lam_init),
        grid=(b, N_GROUPS, t // blk),
        in_specs=[pl.BlockSpec(memory_space=pltpu.SMEM), vec, vec, vec, vec,
                  pl.BlockSpec((1, DF_V_DIM), fixed),
                  qspec, feat, feat, qspec, rows, rows],
        out_specs=[qspec, qspec],
        out_shape=[jax.ShapeDtypeStruct((b, t, SB_WIDTH), BF16),
                   jax.ShapeDtypeStruct((b, t, DF_WIDTH), BF16)],
        scratch_shapes=[pltpu.VMEM((2, blk, SB_HEAD_DIM), F32),
                        pltpu.VMEM((2, blk, 1), F32),
                        pltpu.VMEM((2, blk, DF_V_DIM), F32),
                        pltpu.VMEM((2, blk, 1), F32),
                        pltpu.VMEM((2, blk, 1), F32)],
        compiler_params=_params("arbitrary", "arbitrary", "arbitrary"),
        name="prompt_attention",
    )(slopes, *lam_vecs, gn, qsb, ksb_t, vsb_t, qdf, kdf, vdf)


def _decode_attn_body(pt_ref, slopes_ref, lq1_ref, lk1_ref, lq2_ref, lk2_ref, gn_ref,
                      qsb_ref, ksb_new_ref, vsb_new_ref, qdf_ref, kdf_new_ref, vdf_new_ref,
                      *rest, lam_init, past_len, n_pages, t_new):
    del pt_ref
    npp = PAGES_PER_STEP
    ksb_pgs, vsb_pgs, kdf_pgs, vdf_pgs = (rest[i * npp:(i + 1) * npp] for i in range(4))
    ysb_ref, ydf_ref, qbd_sb, q_df, acc_sb, run_sb, acc_df, m_df, l_df = rest[4 * npp:]
    step = pl.program_id(1)
    rows = SB_HEADS * t_new
    df_rows = 2 * t_new
    pg = PAGE_SIZE

    row_i = lax.broadcasted_iota(jnp.int32, (rows, pg), 0)
    col_i = lax.broadcasted_iota(jnp.int32, (rows, pg), 1)
    tok = row_i % t_new
    u = _suffix_ones(pg)
    df_head = lax.broadcasted_iota(jnp.int32, (rows, 1), 0) // df_rows
    slope = jnp.zeros((rows, 1), F32)
    for h in range(DF_HEADS):
        slope = jnp.where(df_head == h, slopes_ref[h], slope)


    @pl.when(step == 0)
    def _():
        r = lax.broadcasted_iota(jnp.int32, (rows, SB_WIDTH), 0) // t_new
        cb = lax.broadcasted_iota(jnp.int32, (rows, SB_WIDTH), 1) // SB_HEAD_DIM
        q = qsb_ref[0].astype(F32)
        qbd_sb[...] = jnp.where(r == cb, jnp.concatenate([q] * SB_HEADS, axis=0),
                                0.0).astype(BF16)
        q = qdf_ref[0].astype(F32)
        first_half = lax.broadcasted_iota(jnp.int32, (t_new, DF_QK_DIM), 1) < DF_HEAD_DIM
        parts = []
        for h in range(DF_HEADS):
            qh = q[:, h * DF_QK_DIM:(h + 1) * DF_QK_DIM]
            parts += [jnp.where(first_half, qh, 0.0), jnp.where(first_half, 0.0, qh)]
        q_df[...] = jnp.concatenate(parts, axis=0).astype(BF16)
        acc_df[...] = jnp.zeros_like(acc_df)
        m_df[...] = jnp.full_like(m_df, NEG_INF)
        l_df[...] = jnp.zeros_like(l_df)

        def padded(ref):
            pad = jnp.zeros((pg - t_new, ref.shape[-1]), F32)
            return jnp.concatenate([ref[0], pad], axis=0).astype(BF16)

        ksb, vsb, kdf, vdf = (padded(ksb_new_ref), padded(vsb_new_ref),
                              padded(kdf_new_ref), padded(vdf_new_ref))
        z = lax.dot_general(qbd_sb[...], ksb, _NT, preferred_element_type=F32)
        terms = _sb_log_terms(z, u, col_i < tok)
        acc_sb[...] = jnp.dot(_sb_weights(terms, 0.0, col_i < tok), vsb,
                              preferred_element_type=F32)
        run_sb[...] = terms[2]
        head = lambda a, h: a[:, h * DF_V_DIM:(h + 1) * DF_V_DIM]
        s = jnp.concatenate(
            [lax.dot_general(q_df[h * df_rows:(h + 1) * df_rows, :], head(kdf, h), _NT,
                             preferred_element_type=F32) for h in range(DF_HEADS)], axis=0)
        s = jnp.where(col_i <= tok, s - slope * (tok - col_i).astype(F32), NEG_INF)
        alpha, p = _softmax_probs(s, m_df, l_df, slice(None))
        acc_df[...] = alpha * acc_df[...] + jnp.concatenate(
            [jnp.dot(p[h * df_rows:(h + 1) * df_rows, :], head(vdf, h),
                     preferred_element_type=F32) for h in range(DF_HEADS)], axis=0)

    first_page = n_pages - 1 - step * npp
    zs = [jnp.dot(qbd_sb[...], ksb_pgs[i][0, 0].astype(BF16), preferred_element_type=F32)
          for i in range(npp)]
    qks = [lax.dot_general(q_df[...], kdf_pgs[i][0, 0].astype(BF16), _NT,
                           preferred_element_type=F32) for i in range(npp)]
    terms = [_sb_log_terms(z, u, None) for z in zs]

    wide = DF_HEADS * pg
    r_w = lax.broadcasted_iota(jnp.int32, (rows, wide), 0)
    c_w = lax.broadcasted_iota(jnp.int32, (rows, wide), 1)
    own_head = (c_w % DF_HEADS) == (r_w // df_rows)
    local = jnp.where(own_head, (-slope) * (r_w % t_new - c_w // DF_HEADS).astype(F32), NEG_INF)
    scores = []
    for i in range(npp):
        page_start = (first_page - i) * pg
        scores.append(qks[i] + (local - slope * (past_len - page_start).astype(F32)))
    alpha, p = _softmax_probs(jnp.concatenate(scores, axis=-1), m_df, l_df, slice(None))

    run = run_sb[...]
    ws = []
    for i in range(npp):
        ws.append(_sb_weights(terms[i], run, None))
        run = run + terms[i][2]
    run_sb[...] = run

    acc_df[...] = alpha * acc_df[...] + sum(
        jnp.dot(p[:, i * wide:(i + 1) * wide], vdf_pgs[i][0, 0].astype(BF16),
                preferred_element_type=F32) for i in range(npp))
    acc_sb[...] += sum(lax.dot_general(ws[i], vsb_pgs[i][0, 0].astype(BF16), _NT,
                                       preferred_element_type=F32) for i in range(npp))

    @pl.when(step == pl.num_programs(1) - 1)
    def _():
        r = lax.broadcasted_iota(jnp.int32, (rows, SB_WIDTH), 0) // t_new
        cb = lax.broadcasted_iota(jnp.int32, (rows, SB_WIDTH), 1) // SB_HEAD_DIM
        ysb = jnp.where(r == cb, acc_sb[...], 0.0).reshape(SB_HEADS, t_new, SB_WIDTH).sum(axis=0)
        ysb_ref[0] = ysb.astype(ysb_ref.dtype)

        lam = _lam(lq1_ref, lk1_ref, lq2_ref, lk2_ref, lam_init)
        normed = acc_df[...] / l_df[...]
        outs = []
        for h in range(DF_HEADS):
            first = normed[h * df_rows:h * df_rows + t_new, :]
            second = normed[h * df_rows + t_new:(h + 1) * df_rows, :]
            o = first - lam * second
            outs.append((_rms(o) * gn_ref[...]) * (1.0 - lam_init))
        ydf_ref[0] = jnp.concatenate(outs, axis=-1).astype(ydf_ref.dtype)


def _decode_attention(page_table, slopes, lam_vecs, gn, qsb, ksb, vsb, qdf, kdf, vdf,
                      cache_sb_k_t, cache_sb_v_t, cache_df_k, cache_df_v, layer, lam_init):
    nb, t_new, _ = qsb.shape
    n_pages = page_table.shape[1]
    past_len = n_pages * PAGE_SIZE
    rows = SB_HEADS * t_new
    new = pl.BlockSpec((1, t_new, SB_WIDTH), lambda bi, s, pt: (bi, 0, 0))
    fixed = lambda bi, s, pt: (0, 0)
    vec = pl.BlockSpec((1, DF_HEAD_DIM), fixed)

    npp = PAGES_PER_STEP

    def page_spec(shape, i):
        return pl.BlockSpec(
            (1, 1) + shape,
            lambda bi, s, pt: (layer, pt[bi, n_pages - 1 - (s * npp + i)], 0, 0))

    sb_pages = [page_spec((SB_WIDTH, PAGE_SIZE), i) for i in range(npp)]
    df_pages = [page_spec((PAGE_SIZE * DF_HEADS, DF_V_DIM), i) for i in range(npp)]
    grid_spec = pltpu.PrefetchScalarGridSpec(
        num_scalar_prefetch=1,
        grid=(nb, n_pages // npp),
        in_specs=[pl.BlockSpec(memory_space=pltpu.SMEM), vec, vec, vec, vec,
                  pl.BlockSpec((1, DF_V_DIM), fixed),
                  new, new, new, new, new, new] + sb_pages + sb_pages + df_pages + df_pages,
        out_specs=[new, new],
        scratch_shapes=[pltpu.VMEM((rows, SB_WIDTH), BF16),
                        pltpu.VMEM((rows, DF_QK_DIM), BF16),
                        pltpu.VMEM((rows, SB_WIDTH), F32),
                        pltpu.VMEM((rows, 1), F32),
                        pltpu.VMEM((rows, DF_V_DIM), F32),
                        pltpu.VMEM((rows, 1), F32),
                        pltpu.VMEM((rows, 1), F32)],
    )
    return pl.pallas_call(
        functools.partial(_decode_attn_body, lam_init=lam_init, past_len=past_len,
                          n_pages=n_pages, t_new=t_new),
        grid_spec=grid_spec,
        out_shape=[jax.ShapeDtypeStruct((nb, t_new, SB_WIDTH), BF16),
                   jax.ShapeDtypeStruct((nb, t_new, DF_WIDTH), BF16)],
        compiler_params=_params("arbitrary", "arbitrary"),
        name="decode_attention",
    )(page_table, slopes, *lam_vecs, gn, qsb, ksb, vsb, qdf, kdf, vdf,
      *([cache_sb_k_t] * npp), *([cache_sb_v_t] * npp),
      *([cache_df_k] * npp), *([cache_df_v] * npp))


def _merge_out_body(x_ref, ysb_ref, ydf_ref, gate_ref, wsb_ref, wdf_ref, wout_ref, o_ref):
    br_sb = jnp.dot(ysb_ref[...], wsb_ref[...], preferred_element_type=F32)
    br_df = jnp.dot(ydf_ref[...], wdf_ref[...], preferred_element_type=F32)
    merged = gate_ref[:, :D_MODEL] * br_sb + gate_ref[:, D_MODEL:] * br_df
    o_ref[...] = x_ref[...] + jnp.dot(merged.astype(BF16), wout_ref[...],
                                      preferred_element_type=F32)


def _merge_out(x, ysb, ydf, gates, wsb_bf, wdf_bf, wout_bf):
    n = x.shape[0]
    tm = min(ROW_TILE, n)
    row = lambda i: (i, 0)
    fixed = lambda i: (0, 0)
    return pl.pallas_call(
        _merge_out_body,
        grid=(n // tm,),
        in_specs=[pl.BlockSpec((tm, D_MODEL), row),
                  pl.BlockSpec((tm, SB_WIDTH), row),
                  pl.BlockSpec((tm, DF_WIDTH), row),
                  pl.BlockSpec((tm, GATE_COLS), row),
                  _resident((SB_WIDTH, D_MODEL), fixed),
                  _resident((DF_WIDTH, D_MODEL), fixed),
                  _resident((D_MODEL, D_MODEL), fixed)],
        out_specs=pl.BlockSpec((tm, D_MODEL), row),
        out_shape=jax.ShapeDtypeStruct((n, D_MODEL), F32),
        compiler_params=_params("arbitrary"),
        name="merge_out",
    )(x, ysb, ydf, gates, wsb_bf, wdf_bf, wout_bf)


def _conv_ffn_body(*refs, seq_len, tile_rows, has_state):
    if has_state:
        (x_ref, g_ref, wup_ref, cw_ref, cb_ref, wdn_ref, e1_ref, e2_ref,
         o_ref, tail_ref) = refs
    else:
        (x_ref, g_ref, wup_ref, cw_ref, cb_ref, wdn_ref,
         o_ref, tail_ref, carry) = refs

        @pl.when(pl.program_id(1) == 0)
        def _():
            carry[...] = jnp.zeros_like(carry)

    x = x_ref[...]
    hb = (_rms(x) * g_ref[...]).astype(BF16)
    row = lax.broadcasted_iota(jnp.int32, (tile_rows, FF_CHUNK), 0)
    t_in_seq = row % seq_len if has_state else row
    n_chunks = D_FF // FF_CHUNK

    def up(ci):
        lo = ci * FF_CHUNK
        return (jnp.dot(hb, wup_ref[:, lo:lo + FF_CHUNK], preferred_element_type=F32),
                jnp.dot(hb, wup_ref[:, D_FF + lo:D_FF + lo + FF_CHUNK],
                        preferred_element_type=F32))

    acc = x
    ahead = up(0)
    for ci in range(n_chunks):
        cols = slice(ci * FF_CHUNK, (ci + 1) * FF_CHUNK)
        a, b = ahead
        if ci + 1 < n_chunks:
            ahead = up(ci + 1)
        a1 = pltpu.roll(a, 1, 0)
        a2 = pltpu.roll(a, 2, 0)
        if has_state:
            a1 = jnp.where(t_in_seq == 0, e1_ref[:, cols], a1)
            a2 = jnp.where(t_in_seq < 2, e2_ref[:, cols], a2)
            tail_ref[:, cols] = a
        else:
            last = carry[SUBLANES - 1:SUBLANES, cols]
            prev = carry[SUBLANES - 2:SUBLANES - 1, cols]
            a1 = jnp.where(t_in_seq == 0, last, a1)
            a2 = jnp.where(t_in_seq == 0, prev, jnp.where(t_in_seq == 1, last, a2))
            tail = a[tile_rows - SUBLANES:, :]
            carry[:, cols] = tail
            tail_ref[0, :, cols] = tail
        conv = cb_ref[:, cols] + cw_ref[0:1, cols] * a2
        conv = conv + cw_ref[1:2, cols] * a1
        conv = conv + cw_ref[2:3, cols] * a
        gated = (jax.nn.gelu(conv) * b).astype(BF16)
        acc = acc + jnp.dot(gated, wdn_ref[cols, :], preferred_element_type=F32)
    o_ref[...] = acc


def _conv_ffn(x, g, wup_bf, conv_w, conv_b, wdn_bf, batch, seq_len, state):
    n = x.shape[0]
    has_state = state is not None
    tm = STATE_ROW_TILE if has_state else ROW_TILE
    body = functools.partial(_conv_ffn_body, seq_len=seq_len, has_state=has_state,
                             tile_rows=tm)
    weights = [g.reshape(1, D_MODEL), wup_bf, conv_w, conv_b.reshape(1, D_FF), wdn_bf]
    if has_state:
        pad = lambda s: jnp.pad(s, ((0, 0), (0, seq_len - s.shape[1]), (0, 0))).reshape(n, D_FF)
        e1 = pad(state[:, 1:2])
        e2 = pad(state)
        row = lambda i: (i, 0)
        fixed = lambda i: (0, 0)
        out, gate_rows = pl.pallas_call(
            body,
            grid=(n // tm,),
            in_specs=[pl.BlockSpec((tm, D_MODEL), row),
                      pl.BlockSpec((1, D_MODEL), fixed),
                      _resident((D_MODEL, 2 * D_FF), fixed),
                      pl.BlockSpec((CONV_WIDTH, D_FF), fixed),
                      pl.BlockSpec((1, D_FF), fixed),
                      _resident((D_FF, D_MODEL), fixed),
                      pl.BlockSpec((tm, D_FF), row),
                      pl.BlockSpec((tm, D_FF), row)],
            out_specs=[pl.BlockSpec((tm, D_MODEL), row), pl.BlockSpec((tm, D_FF), row)],
            out_shape=[jax.ShapeDtypeStruct((n, D_MODEL), F32),
                       jax.ShapeDtypeStruct((n, D_FF), F32)],
            compiler_params=_params("arbitrary"),
            name="conv_ffn_decode",
        )(x, *weights, e1, e2)
        tail = gate_rows.reshape(batch, seq_len, D_FF)[:, seq_len - (CONV_WIDTH - 1):]
        return out, tail
    tiles = seq_len // tm
    row = lambda bi, ti: (bi * tiles + ti, 0)
    fixed = lambda bi, ti: (0, 0)
    out, tail8 = pl.pallas_call(
        body,
        grid=(batch, tiles),
        in_specs=[pl.BlockSpec((tm, D_MODEL), row),
                  pl.BlockSpec((1, D_MODEL), fixed),
                  _resident((D_MODEL, 2 * D_FF), fixed),
                  pl.BlockSpec((CONV_WIDTH, D_FF), fixed),
                  pl.BlockSpec((1, D_FF), fixed),
                  _resident((D_FF, D_MODEL), fixed)],
        out_specs=[pl.BlockSpec((tm, D_MODEL), row),
                   pl.BlockSpec((1, SUBLANES, D_FF), lambda bi, ti: (bi, 0, 0))],
        out_shape=[jax.ShapeDtypeStruct((n, D_MODEL), F32),
                   jax.ShapeDtypeStruct((batch, SUBLANES, D_FF), F32)],
        scratch_shapes=[pltpu.VMEM((SUBLANES, D_FF), F32)],
        compiler_params=_params("arbitrary", "arbitrary"),
        name="conv_ffn_prompt",
    )(x, *weights)
    return out, tail8[:, SUBLANES - (CONV_WIDTH - 1):]


def _final_norm_body(x_ref, g_ref, o_ref):
    o_ref[...] = _rms(x_ref[...]) * g_ref[...]


def _final_norm(x, g):
    n = x.shape[0]
    tm = min(ROW_TILE, n)
    return pl.pallas_call(
        _final_norm_body,
        grid=(n // tm,),
        in_specs=[pl.BlockSpec((tm, D_MODEL), lambda i: (i, 0)),
                  pl.BlockSpec((1, D_MODEL), lambda i: (0, 0))],
        out_specs=pl.BlockSpec((tm, D_MODEL), lambda i: (i, 0)),
        out_shape=jax.ShapeDtypeStruct((n, D_MODEL), F32),
        compiler_params=_params("arbitrary"),
        name="final_norm",
    )(x, g.reshape(1, D_MODEL))


def _layer_consts(wts, l):
    lam_init = 0.8 - 0.6 * math.exp(-0.3 * l)
    lam_vecs = [wts[k][l].reshape(1, DF_HEAD_DIM)
                for k in ("df_lam_q1", "df_lam_k1", "df_lam_q2", "df_lam_k2")]
    return lam_init, lam_vecs, wts["df_norm_g"][l].reshape(1, DF_V_DIM)


def _mix_and_ffn(xf, ysb, ydf, gates, wts, l, batch, seq_len, state):
    n = xf.shape[0]
    xf = _merge_out(xf, ysb.reshape(n, SB_WIDTH), ydf.reshape(n, DF_WIDTH), gates,
                    wts["w_br_sb"][l], wts["w_br_df"][l], wts["w_out"][l])
    return _conv_ffn(xf, wts["ffn_norm_g"][l], wts["w_up"][l], wts["conv_w"][l],
                     wts["conv_b"][l], wts["w_down"][l], batch, seq_len, state)


def _prompt_trunk(x, slopes, wts):
    batch, seq_len, _ = x.shape
    xf = x.reshape(batch * seq_len, D_MODEL)
    rows = ([], [], [], [])
    convs = []
    for l in range(DEPTH):
        lam_init, lam_vecs, gn = _layer_consts(wts, l)
        (qsb, ksb_t, vsb_t, ksb_tb, vsb_tb, qdf, kdf, vdf, kdf_b, vdf_b, gates) = _in_proj_prompt(
            xf, wts["attn_norm_g"][l], wts["w_in"][l], wts["b_gate"][l], batch, seq_len)
        tok = lambda a: a.reshape(batch, seq_len, SB_WIDTH)
        ysb, ydf = _prompt_attention(slopes, lam_vecs, gn, tok(qsb), ksb_tb, vsb_tb,
                                     tok(qdf), tok(kdf_b), tok(vdf_b), lam_init)
        xf, tail = _mix_and_ffn(xf, ysb, ydf, gates, wts, l, batch, seq_len, None)
        for lst, a in zip(rows, (ksb_t, vsb_t, kdf, vdf)):
            lst.append(a)
        convs.append(tail)
    y = _final_norm(xf, wts["final_norm_g"]).reshape(batch, seq_len, D_MODEL)
    to_tokens = lambda a: a.reshape(DEPTH, batch, SB_HEADS, SB_HEAD_DIM, seq_len).transpose(0, 1, 4, 2, 3)
    return (y, to_tokens(jnp.stack(rows[0])), to_tokens(jnp.stack(rows[1])),
            jnp.stack(rows[2]), jnp.stack(rows[3]), jnp.stack(convs))


def _decode_trunk(x, caches, conv_state, page_table, slopes, wts):
    batch, seq_len, _ = x.shape
    n = batch * seq_len
    xf = x.reshape(n, D_MODEL)
    depth, n_phys = caches[0].shape[:2]
    sb_t = lambda c: c.transpose(0, 1, 3, 4, 2).reshape(depth, n_phys, SB_WIDTH, PAGE_SIZE)
    df_rows = lambda c: c.reshape(depth, n_phys, PAGE_SIZE * DF_HEADS, DF_V_DIM)
    caches = (sb_t(caches[0]), sb_t(caches[1]), df_rows(caches[2]), df_rows(caches[3]))
    rows = ([], [], [], [])
    convs = []
    for l in range(DEPTH):
        lam_init, lam_vecs, gn = _layer_consts(wts, l)
        qsb, ksb, vsb, qdf, kdf, vdf, gates = _in_proj(
            xf, wts["attn_norm_g"][l], wts["w_in"][l], wts["b_gate"][l])
        tok = lambda a: a.reshape(batch, seq_len, SB_WIDTH)
        ysb, ydf = _decode_attention(page_table, slopes, lam_vecs, gn, tok(qsb), tok(ksb),
                                     tok(vsb), tok(qdf), tok(kdf), tok(vdf), *caches, l, lam_init)
        xf, tail = _mix_and_ffn(xf, ysb, ydf, gates, wts, l, batch, seq_len, conv_state[l])
        rows[0].append(ksb.reshape(batch, seq_len, SB_HEADS, SB_HEAD_DIM))
        rows[1].append(vsb.reshape(batch, seq_len, SB_HEADS, SB_HEAD_DIM))
        rows[2].append(kdf.reshape(batch, seq_len, DF_HEADS, DF_QK_DIM))
        rows[3].append(vdf.reshape(batch, seq_len, DF_HEADS, DF_V_DIM))
        convs.append(tail)
    y = _final_norm(xf, wts["final_norm_g"]).reshape(batch, seq_len, D_MODEL)
    return (y, *[jnp.stack(r) for r in rows], jnp.stack(convs))


def kernel(x_prompt, x_sample, cache_sb_k, cache_sb_v, cache_df_k, cache_df_v, state_ffn_conv,
           page_table, attn_norm_g, w_in, b_gate, df_lam_q1, df_lam_k1, df_lam_q2, df_lam_k2,
           df_norm_g, w_br_sb, w_br_df, w_out, ffn_norm_g, w_up, conv_w, conv_b, w_down,
           final_norm_g):
    wts = dict(attn_norm_g=attn_norm_g, w_in=w_in.astype(BF16), b_gate=b_gate,
               df_lam_q1=df_lam_q1, df_lam_k1=df_lam_k1, df_lam_q2=df_lam_q2,
               df_lam_k2=df_lam_k2, df_norm_g=df_norm_g,
               w_br_sb=w_br_sb.astype(BF16), w_br_df=w_br_df.astype(BF16),
               w_out=w_out.astype(BF16), ffn_norm_g=ffn_norm_g, w_up=w_up.astype(BF16),
               conv_w=conv_w, conv_b=conv_b, w_down=w_down.astype(BF16),
               final_norm_g=final_norm_g)
    slopes = 2.0 ** (-8.0 * jnp.arange(1, DF_HEADS + 1, dtype=F32) / DF_HEADS)
    caches = (cache_sb_k, cache_sb_v, cache_df_k, cache_df_v)
    y_p, *rest_p = _prompt_trunk(x_prompt, slopes, wts)
    y_s, *rest_s = _decode_trunk(x_sample, caches, state_ffn_conv, page_table, slopes, wts)
    return (y_p, y_s, *rest_p, *rest_s)
```
